```python
import jax, jax.numpy as jnp
from jax import lax
import numpy as np

D_MODEL = 1024
BATCH = 16
SEQ = 4096
DEPTH = 4
DEC_BATCH = 8
DEC_SEQ = 8192
PAST_LEN = 128

N_MIXERS = 2
HEAD_DIM = 64
N_Q_HEADS = D_MODEL // HEAD_DIM
N_KV_HEADS = N_Q_HEADS // 4
Q_PER_KV = N_Q_HEADS // N_KV_HEADS
QKV_DIM = (N_Q_HEADS + 2 * N_KV_HEADS) * HEAD_DIM
ROPE_THETA = 10000.0
GRID_W = 64
Q_BLOCK = 128
POOL_WINDOWS = (2, 4, 8, 16)
N_POOL_GROUPS = len(POOL_WINDOWS)
POOL_GROUP_DIM = D_MODEL // N_POOL_GROUPS
D_FF = ((8 * D_MODEL // 3 + 127) // 128) * 128
N_ATTN_LAYERS = (DEPTH + 1) // 2
N_POOL_LAYERS = DEPTH // 2
EPS = 1e-6

kernel_name = "hybrid_gqa_axialrope_multiscale_pool_macaron_encoder"


def rmsnorm(x, g):
    xf = x.astype(jnp.float32)
    r = lax.rsqrt(jnp.mean(xf * xf, axis=-1, keepdims=True) + EPS)
    return (xf * r).astype(x.dtype) * g.astype(x.dtype)


def axial_rope_tables(S):
    rows = S // GRID_W
    row = jnp.repeat(jnp.arange(rows, dtype=jnp.float32), GRID_W)
    col = jnp.tile(jnp.arange(GRID_W, dtype=jnp.float32), rows)
    n_freq = HEAD_DIM // 4
    freqs = ROPE_THETA ** (-jnp.arange(n_freq, dtype=jnp.float32) / n_freq)
    ang = jnp.concatenate([row[:, None] * freqs, col[:, None] * freqs], axis=-1)
    return jnp.cos(ang), jnp.sin(ang)


def apply_rope(x, cos, sin):
    xp = x.astype(jnp.float32).reshape(*x.shape[:-1], HEAD_DIM // 2, 2)
    x1, x2 = xp[..., 0], xp[..., 1]
    c = cos[None, :, None, :]
    s = sin[None, :, None, :]
    out = jnp.stack([x1 * c - x2 * s, x1 * s + x2 * c], axis=-1)
    return out.reshape(x.shape).astype(x.dtype)


def swiglu(h, w_gate, w_up, w_down):
    return (jax.nn.silu(h @ w_gate) * (h @ w_up)) @ w_down


def attention_mixer(h, w_qkv, q_gain, k_gain, w_o, cos, sin):
    B, S, _ = h.shape
    qkv = h @ w_qkv
    nq = N_Q_HEADS * HEAD_DIM
    nk = N_KV_HEADS * HEAD_DIM
    q = qkv[..., :nq].reshape(B, S, N_Q_HEADS, HEAD_DIM)
    k = qkv[..., nq:nq + nk].reshape(B, S, N_KV_HEADS, HEAD_DIM)
    v = qkv[..., nq + nk:].reshape(B, S, N_KV_HEADS, HEAD_DIM)
    q = apply_rope(rmsnorm(q, q_gain), cos, sin) * (HEAD_DIM ** -0.5)
    k = apply_rope(rmsnorm(k, k_gain), cos, sin)
    q = q.reshape(B, S // Q_BLOCK, Q_BLOCK, N_KV_HEADS, Q_PER_KV, HEAD_DIM)
    qb = jnp.moveaxis(q, 1, 0)

    def block(qblk):
        s = jnp.einsum('bqgrd,bkgd->bgrqk', qblk, k).astype(jnp.float32)
        p = jax.nn.softmax(s, axis=-1).astype(v.dtype)
        return jnp.einsum('bgrqk,bkgd->bqgrd', p, v)

    o = lax.map(block, qb)
    o = jnp.moveaxis(o, 0, 1).reshape(B, S, N_Q_HEADS * HEAD_DIM)
    return o @ w_o


def pool_mixer(h, w_in, w_group, w_out, scale):
    B, S, _ = h.shape
    u = h @ w_in
    uf = u.astype(jnp.float32)
    cs = jnp.concatenate([jnp.zeros((B, 1, D_MODEL), jnp.float32), jnp.cumsum(uf, axis=1)], axis=1)
    t = jnp.arange(S)
    outs = []
    for g, w in enumerate(POOL_WINDOWS):
        lo = jnp.clip(t - w // 2, 0, S)
        hi = jnp.clip(t + w // 2, 0, S)
        sl = slice(g * POOL_GROUP_DIM, (g + 1) * POOL_GROUP_DIM)
        cg = cs[..., sl]
        cnt = (hi - lo).astype(jnp.float32)[None, :, None]
        mean = (jnp.take(cg, hi, axis=1) - jnp.take(cg, lo, axis=1)) / cnt
        outs.append(mean - uf[..., sl])
    d = jnp.stack(outs, axis=2).astype(h.dtype)
    z = jnp.einsum('bsgc,gcd->bsgd', d, w_group).reshape(B, S, D_MODEL)
    return (z @ w_out) * scale


def run_trunk(x, norm_gains, ffn_w_gate, ffn_w_up, ffn_w_down, attn_w_qkv, attn_q_gain,
              attn_k_gain, attn_w_o, pool_w_in, pool_w_group, pool_w_out, pool_scale):
    cos, sin = axial_rope_tables(x.shape[1])
    for i in range(DEPTH):
        x = x + 0.5 * swiglu(rmsnorm(x, norm_gains[i, 0]), ffn_w_gate[i, 0], ffn_w_up[i, 0], ffn_w_down[i, 0])
        h = rmsnorm(x, norm_gains[i, 1])
        j = i // N_MIXERS
        if i % N_MIXERS == 0:
            x = x + attention_mixer(h, attn_w_qkv[j], attn_q_gain[j], attn_k_gain[j], attn_w_o[j], cos, sin)
        else:
            x = x + pool_mixer(h, pool_w_in[j], pool_w_group[j], pool_w_out[j], pool_scale[j])
        x = x + 0.5 * swiglu(rmsnorm(x, norm_gains[i, 2]), ffn_w_gate[i, 1], ffn_w_up[i, 1], ffn_w_down[i, 1])
    return x


def setup_inputs(seed: int = 0) -> dict:
    key = jax.random.key(seed)
    ks = jax.random.split(key, 16)
    f32 = jnp.float32
    nrm = lambda k, shape, s: jax.random.normal(k, shape, f32) * s
    return {
        "x_prompt": nrm(ks[0], (BATCH, SEQ, D_MODEL), 1.0),
        "x_sample": nrm(ks[1], (DEC_BATCH, DEC_SEQ, D_MODEL), 1.0),
        "norm_gains": 1.0 + nrm(ks[2], (DEPTH, 3, D_MODEL), 0.05),
        "ffn_w_gate": nrm(ks[3], (DEPTH, 2, D_MODEL, D_FF), D_MODEL ** -0.5),
        "ffn_w_up": nrm(ks[4], (DEPTH, 2, D_MODEL, D_FF), D_MODEL ** -0.5),
        "ffn_w_down": nrm(ks[5], (DEPTH, 2, D_FF, D_MODEL), D_FF ** -0.5),
        "attn_w_qkv": nrm(ks[6], (N_ATTN_LAYERS, D_MODEL, QKV_DIM), D_MODEL ** -0.5),
        "attn_q_gain": 1.0 + nrm(ks[7], (N_ATTN_LAYERS, HEAD_DIM), 0.05),
        "attn_k_gain": 1.0 + nrm(ks[8], (N_ATTN_LAYERS, HEAD_DIM), 0.05),
        "attn_w_o": nrm(ks[9], (N_ATTN_LAYERS, N_Q_HEADS * HEAD_DIM, D_MODEL), D_MODEL ** -0.5),
        "pool_w_in": nrm(ks[10], (N_POOL_LAYERS, D_MODEL, D_MODEL), D_MODEL ** -0.5),
        "pool_w_group": nrm(ks[11], (N_POOL_LAYERS, N_POOL_GROUPS, POOL_GROUP_DIM, POOL_GROUP_DIM), POOL_GROUP_DIM ** -0.5),
        "pool_w_out": nrm(ks[12], (N_POOL_LAYERS, D_MODEL, D_MODEL), D_MODEL ** -0.5),
        "pool_scale": 0.5 + nrm(ks[13], (N_POOL_LAYERS, D_MODEL), 0.05),
    }


def reference(x_prompt, x_sample, norm_gains, ffn_w_gate, ffn_w_up, ffn_w_down, attn_w_qkv,
              attn_q_gain, attn_k_gain, attn_w_o, pool_w_in, pool_w_group, pool_w_out, pool_scale):
    y_prompt = run_trunk(x_prompt, norm_gains, ffn_w_gate, ffn_w_up, ffn_w_down, attn_w_qkv, attn_q_gain,
                         attn_k_gain, attn_w_o, pool_w_in, pool_w_group, pool_w_out, pool_scale)
    y_sample = run_trunk(x_sample, norm_gains, ffn_w_gate, ffn_w_up, ffn_w_down, attn_w_qkv, attn_q_gain,
                         attn_k_gain, attn_w_o, pool_w_in, pool_w_group, pool_w_out, pool_scale)
    return (y_prompt, y_sample)
```

```python
import functools

import jax
import jax.numpy as jnp
from jax import lax
from jax.experimental import pallas as pl
from jax.experimental.pallas import tpu as pltpu

D_MODEL = 1024
DEPTH = 4
HEAD_DIM = 64
N_Q_HEADS = D_MODEL // HEAD_DIM
N_KV_HEADS = N_Q_HEADS // 4
Q_PER_KV = N_Q_HEADS // N_KV_HEADS
ROPE_THETA = 10000.0
GRID_W = 64
POOL_WINDOWS = (2, 4, 8, 16)
POOL_GROUP_DIM = D_MODEL // len(POOL_WINDOWS)
D_FF = ((8 * D_MODEL // 3 + 127) // 128) * 128
EPS = 1e-6

LANES = 128
SUBLANES = 8
BF16_ROWS = 16
VMEM_BUDGET = 56 * 1024 * 1024

KV_LANES = LANES
VT_ROWS = HEAD_DIM + BF16_ROWS
HALO = SUBLANES
FF_CHUNK = 256

F32 = jnp.float32
BF16 = jnp.bfloat16


def _tile(n, pref):
    return pref if n % pref == 0 else n


def _params(n_axes):
    return pltpu.CompilerParams(
        dimension_semantics=("arbitrary",) * n_axes, vmem_limit_bytes=VMEM_BUDGET)


def _rmsnorm(x, g):
    r = lax.rsqrt(jnp.mean(x * x, axis=-1, keepdims=True) + EPS)
    return (x * r) * g


def _const_spec(shape, index):
    return pl.BlockSpec(shape, lambda *_: index, pipeline_mode=pl.Buffered(1))


def _ffn_kernel(x_ref, g_ref, wg_ref, wu_ref, wd_ref, o_ref, h_ref, a_ref):
    x = x_ref[...]
    h_ref[...] = _rmsnorm(x, g_ref[...]).astype(BF16)
    for c in range(D_FF // FF_CHUNK):
        sl = slice(c * FF_CHUNK, (c + 1) * FF_CHUNK)
        h = h_ref[...]
        gate = jnp.dot(h, wg_ref[:, sl], preferred_element_type=F32)
        up = jnp.dot(h, wu_ref[:, sl], preferred_element_type=F32)
        a_ref[:, sl] = (gate * jax.nn.sigmoid(gate) * up).astype(BF16)
    y = jnp.dot(a_ref[...], wd_ref[...], preferred_element_type=F32)
    o_ref[...] = x + 0.5 * y


def _ffn(x, gains, wg, wu, wd, layer, which, norm_idx):
    m = x.shape[0]
    tm = _tile(m, 512)
    return pl.pallas_call(
        _ffn_kernel,
        grid=(m // tm,),
        in_specs=[
            pl.BlockSpec((tm, D_MODEL), lambda i: (i, 0)),
            _const_spec((None, None, 1, D_MODEL), (layer, norm_idx, 0, 0)),
            _const_spec((None, None, D_MODEL, D_FF), (layer, which, 0, 0)),
            _const_spec((None, None, D_MODEL, D_FF), (layer, which, 0, 0)),
            _const_spec((None, None, D_FF, D_MODEL), (layer, which, 0, 0)),
        ],
        out_specs=pl.BlockSpec((tm, D_MODEL), lambda i: (i, 0)),
        out_shape=jax.ShapeDtypeStruct((m, D_MODEL), F32),
        scratch_shapes=[pltpu.VMEM((tm, D_MODEL), BF16), pltpu.VMEM((tm, D_FF), BF16)],
        compiler_params=_params(1),
        name="ffn",
    )(x, gains, wg, wu, wd)


def _head_norm_rope(blk, gain, cos, sin, lane):
    sq = blk * blk
    low = lane < HEAD_DIM
    ss_lo = jnp.sum(jnp.where(low, sq, 0.0), axis=1, keepdims=True)
    ss_hi = jnp.sum(jnp.where(low, 0.0, sq), axis=1, keepdims=True)
    r = lax.rsqrt(jnp.where(low, ss_lo, ss_hi) * (1.0 / HEAD_DIM) + EPS)
    xn = (blk * r) * gain
    partner = jnp.where(lane % 2 == 0,
                        pltpu.roll(xn, LANES - 1, 1),
                        pltpu.roll(xn, 1, 1))
    return xn * cos + partner * sin


def _qkv_kernel(x_ref, g_ref, w_ref, qg_ref, kg_ref, cos_ref, sin_ref,
                qt_ref, k_ref, vt_ref):
    tm = x_ref.shape[0]
    h = _rmsnorm(x_ref[...], g_ref[...]).astype(BF16)
    qkv = jnp.dot(h, w_ref[...], preferred_element_type=F32)
    lane = lax.broadcasted_iota(jnp.int32, (tm, LANES), 1)
    cos = cos_ref[...]
    sin = sin_ref[...]
    for j in range(D_MODEL // LANES):
        blk = _head_norm_rope(qkv[:, j * LANES:(j + 1) * LANES], qg_ref[...], cos, sin, lane)
        qt_ref[j * LANES:(j + 1) * LANES, :] = ((blk * HEAD_DIM ** -0.5).T).astype(BF16)
    k0 = D_MODEL
    for g in range(N_KV_HEADS):
        blk = _head_norm_rope(qkv[:, k0 + g * LANES:k0 + (g + 1) * LANES], kg_ref[...], cos, sin, lane)
        k_ref[:, g * KV_LANES:(g + 1) * KV_LANES] = blk.astype(BF16)
    v0 = D_MODEL + N_KV_HEADS * KV_LANES
    row = lax.broadcasted_iota(jnp.int32, (BF16_ROWS, tm), 0)
    ones_rows = jnp.where(row == 0, 1.0, 0.0).astype(BF16)
    for j in range(N_KV_HEADS * HEAD_DIM // LANES):
        vt = (qkv[:, v0 + j * LANES:v0 + (j + 1) * LANES].T).astype(BF16)
        for e in range(2):
            g = 2 * j + e
            vt_ref[g * VT_ROWS:g * VT_ROWS + HEAD_DIM, :] = vt[e * HEAD_DIM:(e + 1) * HEAD_DIM, :]
            vt_ref[g * VT_ROWS + HEAD_DIM:(g + 1) * VT_ROWS, :] = ones_rows


def _qkv(x, gains, layer, w, qg, kg, cos, sin, seq):
    m = x.shape[0]
    tm = _tile(seq, 512)
    per_seq = seq // tm
    n_w = w.shape[1]
    return pl.pallas_call(
        _qkv_kernel,
        grid=(m // tm,),
        in_specs=[
            pl.BlockSpec((tm, D_MODEL), lambda i: (i, 0)),
            _const_spec((None, None, 1, D_MODEL), (layer, 1, 0, 0)),
            _const_spec((D_MODEL, n_w), (0, 0)),
            _const_spec((1, LANES), (0, 0)),
            _const_spec((1, LANES), (0, 0)),
            pl.BlockSpec((tm, LANES), lambda i: (i % per_seq, 0)),
            pl.BlockSpec((tm, LANES), lambda i: (i % per_seq, 0)),
        ],
        out_specs=[
            pl.BlockSpec((D_MODEL, tm), lambda i: (0, i)),
            pl.BlockSpec((tm, N_KV_HEADS * KV_LANES), lambda i: (i, 0)),
            pl.BlockSpec((N_KV_HEADS * VT_ROWS, tm), lambda i: (0, i)),
        ],
        out_shape=[
            jax.ShapeDtypeStruct((D_MODEL, m), BF16),
            jax.ShapeDtypeStruct((m, N_KV_HEADS * KV_LANES), BF16),
            jax.ShapeDtypeStruct((N_KV_HEADS * VT_ROWS, m), BF16),
        ],
        compiler_params=_params(1),
        name="qkv",
    )(x, gains, w, qg, kg, cos, sin)


def _attn_kernel(qt_ref, k_ref, vt_ref, o_ref, m_ref, acc_ref, *, tk):
    seq = k_ref.shape[0]
    m_ref[...] = jnp.full(m_ref.shape, -jnp.inf, F32)
    acc_ref[...] = jnp.zeros(acc_ref.shape, F32)

    def chunk(c, carry):
        k0 = pl.multiple_of(c * tk, tk)
        kc = k_ref[pl.ds(k0, tk), 0:HEAD_DIM]
        vc = vt_ref[:, pl.ds(k0, tk)]
        for h in range(Q_PER_KV):
            qh = qt_ref[h * HEAD_DIM:(h + 1) * HEAD_DIM, :]
            s = jnp.dot(kc, qh, preferred_element_type=F32)
            m_old = m_ref[h]
            m_new = jnp.maximum(m_old, jnp.max(s, axis=0, keepdims=True))
            p = jnp.exp(s - m_new).astype(BF16)
            alpha = jnp.exp(m_old - m_new)
            acc_ref[h] = alpha * acc_ref[h] + jnp.dot(vc, p, preferred_element_type=F32)
            m_ref[h] = m_new
        return carry

    lax.fori_loop(0, seq // tk, chunk, 0)
    for j in range(Q_PER_KV // 2):
        pair = []
        for h in (2 * j, 2 * j + 1):
            a = acc_ref[h]
            pair.append(a[0:HEAD_DIM, :] / a[HEAD_DIM:HEAD_DIM + 1, :])
        o_ref[:, j * LANES:(j + 1) * LANES] = jnp.concatenate(pair, axis=0).T.astype(BF16)


def _attention(qt, k, vt, batch, seq):
    m = batch * seq
    tq = _tile(seq, 512)
    tk = _tile(seq, 512)
    per_seq = seq // tq
    qw = Q_PER_KV * HEAD_DIM
    return pl.pallas_call(
        functools.partial(_attn_kernel, tk=tk),
        grid=(batch, N_KV_HEADS, per_seq),
        in_specs=[
            pl.BlockSpec((qw, tq), lambda b, g, i: (g, b * per_seq + i)),
            pl.BlockSpec((seq, KV_LANES), lambda b, g, i: (b, g)),
            pl.BlockSpec((VT_ROWS, seq), lambda b, g, i: (g, b)),
        ],
        out_specs=pl.BlockSpec((tq, qw), lambda b, g, i: (b * per_seq + i, g)),
        out_shape=jax.ShapeDtypeStruct((m, D_MODEL), BF16),
        scratch_shapes=[pltpu.VMEM((Q_PER_KV, 1, tq), F32),
                        pltpu.VMEM((Q_PER_KV, VT_ROWS, tq), F32)],
        compiler_params=_params(3),
        name="attn",
    )(qt, k, vt)


def _oproj_kernel(x_ref, o_ref, w_ref, y_ref):
    y_ref[...] = x_ref[...] + jnp.dot(o_ref[...], w_ref[...], preferred_element_type=F32)


def _oproj(x, o, w, layer):
    m = x.shape[0]
    tm = _tile(m, 1024)
    return pl.pallas_call(
        _oproj_kernel,
        grid=(m // tm,),
        in_specs=[
            pl.BlockSpec((tm, D_MODEL), lambda i: (i, 0)),
            pl.BlockSpec((tm, D_MODEL), lambda i: (i, 0)),
            _const_spec((None, D_MODEL, D_MODEL), (layer, 0, 0)),
        ],
        out_specs=pl.BlockSpec((tm, D_MODEL), lambda i: (i, 0)),
        out_shape=jax.ShapeDtypeStruct((m, D_MODEL), F32),
        compiler_params=_params(1),
        name="oproj",
    )(x, o, w)


def _pool_kernel(xp_ref, x_ref, xn_ref, g_ref, win_ref, wgrp_ref, wout_ref, scale_ref,
                 o_ref, xe_ref, ue_ref, z_ref, *, seq):
    tm = x_ref.shape[0]
    s0 = (pl.program_id(0) % (seq // tm)) * tm
    xe_ref[0:HALO, :] = xp_ref[...]
    xe_ref[HALO:HALO + tm, :] = x_ref[...]
    xe_ref[HALO + tm:, :] = xn_ref[...]
    h = _rmsnorm(xe_ref[...], g_ref[...]).astype(BF16)
    u = jnp.dot(h, win_ref[...], preferred_element_type=F32)
    row = lax.broadcasted_iota(jnp.int32, (tm + 2 * HALO, 1), 0)
    inside = jnp.logical_and(jnp.logical_or(row >= HALO, s0 > 0),
                             jnp.logical_or(row < HALO + tm, s0 + tm < seq))
    ue_ref[...] = jnp.where(inside, u, 0.0)
    t = s0 + lax.broadcasted_iota(jnp.int32, (tm, 1), 0)
    for g, w in enumerate(POOL_WINDOWS):
        cols = slice(g * POOL_GROUP_DIM, (g + 1) * POOL_GROUP_DIM)
        total = ue_ref[HALO - w // 2:HALO - w // 2 + tm, cols]
        for k in range(-w // 2 + 1, w // 2):
            total = total + ue_ref[HALO + k:HALO + k + tm, cols]
        cnt = (jnp.minimum(t + w // 2, seq) - jnp.maximum(t - w // 2, 0)).astype(F32)
        d = total / cnt - ue_ref[HALO:HALO + tm, cols]
        z = jnp.dot(d.astype(BF16), wgrp_ref[g], preferred_element_type=F32)
        z_ref[:, cols] = z.astype(BF16)
    y = jnp.dot(z_ref[...], wout_ref[...], preferred_element_type=F32)
    o_ref[...] = x_ref[...] + y * scale_ref[...]


def _pool(x, gains, layer, w_in, w_grp, w_out, scale, j, seq):
    m = x.shape[0]
    tm = _tile(seq, 512)
    hb = tm // HALO
    last = m // HALO - 1
    return pl.pallas_call(
        functools.partial(_pool_kernel, seq=seq),
        grid=(m // tm,),
        in_specs=[
            pl.BlockSpec((HALO, D_MODEL), lambda i: (jnp.maximum(i * hb - 1, 0), 0)),
            pl.BlockSpec((tm, D_MODEL), lambda i: (i, 0)),
            pl.BlockSpec((HALO, D_MODEL), lambda i: (jnp.minimum((i + 1) * hb, last), 0)),
            _const_spec((None, None, 1, D_MODEL), (layer, 1, 0, 0)),
            _const_spec((None, D_MODEL, D_MODEL), (j, 0, 0)),
            _const_spec((None, len(POOL_WINDOWS), POOL_GROUP_DIM, POOL_GROUP_DIM), (j, 0, 0, 0)),
            _const_spec((None, D_MODEL, D_MODEL), (j, 0, 0)),
            _const_spec((None, 1, D_MODEL), (j, 0, 0)),
        ],
        out_specs=pl.BlockSpec((tm, D_MODEL), lambda i: (i, 0)),
        out_shape=jax.ShapeDtypeStruct((m, D_MODEL), F32),
        scratch_shapes=[pltpu.VMEM((tm + 2 * HALO, D_MODEL), F32),
                        pltpu.VMEM((tm + 2 * HALO, D_MODEL), F32),
                        pltpu.VMEM((tm, D_MODEL), BF16)],
        compiler_params=_params(1),
        name="pool",
    )(x, x, x, gains, w_in, w_grp, w_out, scale)


def _rope_tables(seq):
    rows = seq // GRID_W
    row = jnp.repeat(jnp.arange(rows, dtype=F32), GRID_W)
    col = jnp.tile(jnp.arange(GRID_W, dtype=F32), rows)
    n_freq = HEAD_DIM // 4
    freqs = ROPE_THETA ** (-jnp.arange(n_freq, dtype=F32) / n_freq)
    ang = jnp.concatenate([row[:, None] * freqs, col[:, None] * freqs], axis=-1)
    lane = jnp.arange(LANES)
    pair = (lane % HEAD_DIM) // 2
    cos = jnp.cos(ang)[:, pair]
    sin = jnp.sin(ang)[:, pair] * jnp.where(lane % 2 == 0, -1.0, 1.0).astype(F32)
    return cos, sin


def _qkv_weight(w):
    nq = N_Q_HEADS * HEAD_DIM
    nk = N_KV_HEADS * HEAD_DIM
    wk = w[:, nq:nq + nk].reshape(D_MODEL, N_KV_HEADS, HEAD_DIM)
    wk = jnp.pad(wk, ((0, 0), (0, 0), (0, KV_LANES - HEAD_DIM))).reshape(D_MODEL, N_KV_HEADS * KV_LANES)
    return jnp.concatenate([w[:, :nq], wk, w[:, nq + nk:]], axis=1).astype(BF16)


def _run_trunk(x, p):
    batch, seq, _ = x.shape
    x = x.reshape(batch * seq, D_MODEL)
    cos, sin = _rope_tables(seq)
    for i in range(DEPTH):
        x = _ffn(x, p["norm"], p["wg"], p["wu"], p["wd"], i, 0, 0)
        j = i // 2
        if i % 2 == 0:
            qt, k, vt = _qkv(x, p["norm"], i, p["wqkv"][j], p["qg"][j], p["kg"][j], cos, sin, seq)
            o = _attention(qt, k, vt, batch, seq)
            x = _oproj(x, o, p["wo"], j)
        else:
            x = _pool(x, p["norm"], i, p["pin"], p["pgrp"], p["pout"], p["pscale"], j, seq)
        x = _ffn(x, p["norm"], p["wg"], p["wu"], p["wd"], i, 1, 2)
    return x.reshape(batch, seq, D_MODEL)


def _prepare(norm_gains, ffn_w_gate, ffn_w_up, ffn_w_down, attn_w_qkv, attn_q_gain, attn_k_gain,
             attn_w_o, pool_w_in, pool_w_group, pool_w_out, pool_scale):
    n_attn = attn_w_qkv.shape[0]
    return {
        "norm": norm_gains.reshape(DEPTH, 3, 1, D_MODEL),
        "wg": ffn_w_gate.astype(BF16),
        "wu": ffn_w_up.astype(BF16),
        "wd": ffn_w_down.astype(BF16),
        "wqkv": [_qkv_weight(attn_w_qkv[j]) for j in range(n_attn)],
        "qg": [jnp.tile(attn_q_gain[j], LANES // HEAD_DIM).reshape(1, LANES) for j in range(n_attn)],
        "kg": [jnp.tile(attn_k_gain[j], LANES // HEAD_DIM).reshape(1, LANES) for j in range(n_attn)],
        "wo": attn_w_o.astype(BF16),
        "pin": pool_w_in.astype(BF16),
        "pgrp": pool_w_group.astype(BF16),
        "pout": pool_w_out.astype(BF16),
        "pscale": pool_scale.reshape(-1, 1, D_MODEL),
    }


def kernel(x_prompt, x_sample, norm_gains, ffn_w_gate, ffn_w_up, ffn_w_down, attn_w_qkv, attn_q_gain, attn_k_gain, attn_w_o, pool_w_in, pool_w_group, pool_w_out, pool_scale):
    p = _prepare(norm_gains, ffn_w_gate, ffn_w_up, ffn_w_down, attn_w_qkv, attn_q_gain, attn_k_gain,
                 attn_w_o, pool_w_in, pool_w_group, pool_w_out, pool_scale)
    return (_run_trunk(x_prompt, p), _run_trunk(x_sample, p))
```

```python
import functools

import jax
import jax.numpy as jnp
from jax import lax
from jax.experimental import pallas as pl
from jax.experimental.pallas import tpu as pltpu

D_MODEL = 1024
DEPTH = 4
HEAD_DIM = 64
N_Q_HEADS = D_MODEL // HEAD_DIM
N_KV_HEADS = N_Q_HEADS // 4
Q_PER_KV = N_Q_HEADS // N_KV_HEADS
ROPE_THETA = 10000.0
GRID_W = 64
POOL_WINDOWS = (2, 4, 8, 16)
POOL_GROUP_DIM = D_MODEL // len(POOL_WINDOWS)
D_FF = ((8 * D_MODEL // 3 + 127) // 128) * 128
EPS = 1e-6
Q_SCALE = HEAD_DIM ** -0.5 * 1.4426950408889634

LANES = 128
SUBLANES = 8
BF16_ROWS = 16
VMEM_BUDGET = 56 * 1024 * 1024

KV_LANES = LANES
VT_ROWS = HEAD_DIM + BF16_ROWS
HALO = SUBLANES
FF_CHUNK = 256
ATTN_TK = 256
ATTN_CHUNKS = 8
ATTN_AHEAD = 2

F32 = jnp.float32
BF16 = jnp.bfloat16


def _tile(n, pref):
    return pref if n % pref == 0 else n


def _params(n_axes):
    return pltpu.CompilerParams(
        dimension_semantics=("arbitrary",) * n_axes, vmem_limit_bytes=VMEM_BUDGET)


def _rmsnorm(x, g):
    r = lax.rsqrt(jnp.mean(x * x, axis=-1, keepdims=True) + EPS)
    return (x * r) * g


def _const_spec(shape, index):
    return pl.BlockSpec(shape, lambda *_: index, pipeline_mode=pl.Buffered(1))


def _ffn_kernel(x_ref, g_ref, wg_ref, wu_ref, wd_ref, o_ref, h_ref, a_ref):
    x = x_ref[...]
    h_ref[...] = _rmsnorm(x, g_ref[...]).astype(BF16)
    for c in range(D_FF // FF_CHUNK):
        sl = slice(c * FF_CHUNK, (c + 1) * FF_CHUNK)
        h = h_ref[...]
        gate = jnp.dot(h, wg_ref[:, sl], preferred_element_type=F32)
        up = jnp.dot(h, wu_ref[:, sl], preferred_element_type=F32)
        a_ref[:, sl] = (gate * jax.nn.sigmoid(gate) * up).astype(BF16)
    y = jnp.dot(a_ref[...], wd_ref[...], preferred_element_type=F32)
    o_ref[...] = x + 0.5 * y


def _ffn(x, gains, wg, wu, wd, layer, which, norm_idx):
    m = x.shape[0]
    tm = _tile(m, 512)
    return pl.pallas_call(
        _ffn_kernel,
        grid=(m // tm,),
        in_specs=[
            pl.BlockSpec((tm, D_MODEL), lambda i: (i, 0)),
            _const_spec((None, None, 1, D_MODEL), (layer, norm_idx, 0, 0)),
            _const_spec((None, None, D_MODEL, D_FF), (layer, which, 0, 0)),
            _const_spec((None, None, D_MODEL, D_FF), (layer, which, 0, 0)),
            _const_spec((None, None, D_FF, D_MODEL), (layer, which, 0, 0)),
        ],
        out_specs=pl.BlockSpec((tm, D_MODEL), lambda i: (i, 0)),
        out_shape=jax.ShapeDtypeStruct((m, D_MODEL), F32),
        scratch_shapes=[pltpu.VMEM((tm, D_MODEL), BF16), pltpu.VMEM((tm, D_FF), BF16)],
        compiler_params=_params(1),
        name="ffn",
    )(x, gains, wg, wu, wd)


def _head_norm_rope(blk, gain, cos, sin, lane):
    sq = blk * blk
    low = lane < HEAD_DIM
    ss_lo = jnp.sum(jnp.where(low, sq, 0.0), axis=1, keepdims=True)
    ss_hi = jnp.sum(jnp.where(low, 0.0, sq), axis=1, keepdims=True)
    r = lax.rsqrt(jnp.where(low, ss_lo, ss_hi) * (1.0 / HEAD_DIM) + EPS)
    xn = (blk * r) * gain
    partner = jnp.where(lane % 2 == 0,
                        pltpu.roll(xn, LANES - 1, 1),
                        pltpu.roll(xn, 1, 1))
    return xn * cos + partner * sin


def _qkv_kernel(x_ref, g_ref, w_ref, qg_ref, kg_ref, cos_ref, sin_ref,
                qt_ref, k_ref, vt_ref, h_ref):
    tm = x_ref.shape[0]
    h_ref[...] = _rmsnorm(x_ref[...], g_ref[...]).astype(BF16)
    lane = lax.broadcasted_iota(jnp.int32, (tm, LANES), 1)
    cos = cos_ref[...]
    sin = sin_ref[...]

    def project(tile):
        cols = slice(tile * 2 * LANES, (tile + 1) * 2 * LANES)
        y = jnp.dot(h_ref[...], w_ref[:, cols], preferred_element_type=F32)
        return y[:, :LANES], y[:, LANES:]

    for t in range(D_MODEL // (2 * LANES)):
        for e, blk in enumerate(project(t)):
            j = 2 * t + e
            blk = _head_norm_rope(blk, qg_ref[...], cos, sin, lane)
            qt_ref[j * LANES:(j + 1) * LANES, :] = ((blk * Q_SCALE).T).astype(BF16)
    t0 = D_MODEL // (2 * LANES)
    for t in range(N_KV_HEADS // 2):
        for e, blk in enumerate(project(t0 + t)):
            g = 2 * t + e
            blk = _head_norm_rope(blk, kg_ref[...], cos, sin, lane)
            k_ref[:, g * KV_LANES:(g + 1) * KV_LANES] = blk.astype(BF16)
    t0 += N_KV_HEADS // 2
    row = lax.broadcasted_iota(jnp.int32, (BF16_ROWS, tm), 0)
    ones_rows = jnp.where(row == 0, 1.0, 0.0).astype(BF16)
    for t in range(N_KV_HEADS * HEAD_DIM // (2 * LANES)):
        for e, blk in enumerate(project(t0 + t)):
            vt = blk.T.astype(BF16)
            for half in range(2):
                g = 2 * (2 * t + e) + half
                vt_ref[g * VT_ROWS:g * VT_ROWS + HEAD_DIM, :] = vt[half * HEAD_DIM:(half + 1) * HEAD_DIM, :]
                vt_ref[g * VT_ROWS + HEAD_DIM:(g + 1) * VT_ROWS, :] = ones_rows


def _qkv(x, gains, layer, w, qg, kg, cos, sin, seq):
    m = x.shape[0]
    tm = _tile(seq, 512)
    per_seq = seq // tm
    n_w = w.shape[1]
    return pl.pallas_call(
        _qkv_kernel,
        grid=(m // tm,),
        in_specs=[
            pl.BlockSpec((tm, D_MODEL), lambda i: (i, 0)),
            _const_spec((None, None, 1, D_MODEL), (layer, 1, 0, 0)),
            _const_spec((D_MODEL, n_w), (0, 0)),
            _const_spec((1, LANES), (0, 0)),
            _const_spec((1, LANES), (0, 0)),
            pl.BlockSpec((tm, LANES), lambda i: (i % per_seq, 0)),
            pl.BlockSpec((tm, LANES), lambda i: (i % per_seq, 0)),
        ],
        out_specs=[
            pl.BlockSpec((D_MODEL, tm), lambda i: (0, i)),
            pl.BlockSpec((tm, N_KV_HEADS * KV_LANES), lambda i: (i, 0)),
            pl.BlockSpec((N_KV_HEADS * VT_ROWS, tm), lambda i: (0, i)),
        ],
        out_shape=[
            jax.ShapeDtypeStruct((D_MODEL, m), BF16),
            jax.ShapeDtypeStruct((m, N_KV_HEADS * KV_LANES), BF16),
            jax.ShapeDtypeStruct((N_KV_HEADS * VT_ROWS, m), BF16),
        ],
        scratch_shapes=[pltpu.VMEM((tm, D_MODEL), BF16)],
        compiler_params=_params(1),
        name="qkv",
    )(x, gains, w, qg, kg, cos, sin)


def _attn_kernel(qt_ref, k_ref, vt_ref, o_ref, m_ref, acc_ref, s_ref, *, tk, chunks_per_step):
    seq = k_ref.shape[0]
    step_keys = chunks_per_step * tk
    m_ref[...] = jnp.full(m_ref.shape, -jnp.inf, F32)
    acc_ref[...] = jnp.zeros(acc_ref.shape, F32)

    def scores(k0, h):
        kc = k_ref[pl.ds(k0, tk), 0:HEAD_DIM]
        qh = qt_ref[h * HEAD_DIM:(h + 1) * HEAD_DIM, :]
        return jnp.dot(kc, qh, preferred_element_type=F32)

    ahead = s_ref.shape[0]
    n_work = chunks_per_step * Q_PER_KV
    for a in range(ahead):
        s_ref[a] = scores((a // Q_PER_KV) * tk, a % Q_PER_KV)

    def step(c, carry):
        base = c * step_keys
        work = [(pl.multiple_of(base + u * tk, tk), h)
                for u in range(chunks_per_step) for h in range(Q_PER_KV)]
        work += [(pl.multiple_of(jnp.minimum(base + step_keys + (a // Q_PER_KV) * tk, seq - tk), tk),
                  a % Q_PER_KV) for a in range(ahead)]
        pending = [s_ref[a] for a in range(ahead)]
        for idx, (k0, h) in enumerate(work[:n_work]):
            pending.append(scores(*work[idx + ahead]))
            s = pending.pop(0)
            vc = vt_ref[:, pl.ds(k0, tk)]
            m_old = m_ref[h]
            m_new = jnp.maximum(m_old, jnp.max(s, axis=0, keepdims=True))
            p = jnp.exp2(s - m_new).astype(BF16)
            alpha = jnp.exp2(m_old - m_new)
            acc_ref[h] = alpha * acc_ref[h] + jnp.dot(vc, p, preferred_element_type=F32)
            m_ref[h] = m_new
        for a in range(ahead):
            s_ref[a] = pending[a]
        return carry

    lax.fori_loop(0, seq // step_keys, step, 0)
    for j in range(Q_PER_KV // 2):
        pair = []
        for h in (2 * j, 2 * j + 1):
            a = acc_ref[h]
            pair.append(a[0:HEAD_DIM, :] / a[HEAD_DIM:HEAD_DIM + 1, :])
        o_ref[:, j * LANES:(j + 1) * LANES] = jnp.concatenate(pair, axis=0).T.astype(BF16)


def _attention(qt, k, vt, batch, seq):
    m = batch * seq
    tq = _tile(seq, 512)
    tk = _tile(seq, ATTN_TK)
    per_seq = seq // tq
    qw = Q_PER_KV * HEAD_DIM
    chunks = ATTN_CHUNKS if seq % (ATTN_CHUNKS * tk) == 0 else 1
    return pl.pallas_call(
        functools.partial(_attn_kernel, tk=tk, chunks_per_step=chunks),
        grid=(batch, N_KV_HEADS, per_seq),
        in_specs=[
            pl.BlockSpec((qw, tq), lambda b, g, i: (g, b * per_seq + i)),
            pl.BlockSpec((seq, KV_LANES), lambda b, g, i: (b, g)),
            pl.BlockSpec((VT_ROWS, seq), lambda b, g, i: (g, b)),
        ],
        out_specs=pl.BlockSpec((tq, qw), lambda b, g, i: (b * per_seq + i, g)),
        out_shape=jax.ShapeDtypeStruct((m, D_MODEL), BF16),
        scratch_shapes=[pltpu.VMEM((Q_PER_KV, 1, tq), F32),
                        pltpu.VMEM((Q_PER_KV, VT_ROWS, tq), F32),
                        pltpu.VMEM((ATTN_AHEAD, tk, tq), F32)],
        compiler_params=_params(3),
        name="attn",
    )(qt, k, vt)


def _oproj_kernel(x_ref, o_ref, w_ref, y_ref):
    y_ref[...] = x_ref[...] + jnp.dot(o_ref[...], w_ref[...], preferred_element_type=F32)


def _oproj(x, o, w, layer):
    m = x.shape[0]
    tm = _tile(m, 1024)
    return pl.pallas_call(
        _oproj_kernel,
        grid=(m // tm,),
        in_specs=[
            pl.BlockSpec((tm, D_MODEL), lambda i: (i, 0)),
            pl.BlockSpec((tm, D_MODEL), lambda i: (i, 0)),
            _const_spec((None, D_MODEL, D_MODEL), (layer, 0, 0)),
        ],
        out_specs=pl.BlockSpec((tm, D_MODEL), lambda i: (i, 0)),
        out_shape=jax.ShapeDtypeStruct((m, D_MODEL), F32),
        compiler_params=_params(1),
        name="oproj",
    )(x, o, w)


def _pool_kernel(xp_ref, x_ref, xn_ref, g_ref, win_ref, wgrp_ref, wout_ref, scale_ref,
                 o_ref, xe_ref, h_ref, ue_ref, z_ref, *, seq):
    tm = x_ref.shape[0]
    s0 = (pl.program_id(0) % (seq // tm)) * tm
    xe_ref[0:HALO, :] = xp_ref[...]
    xe_ref[HALO:HALO + tm, :] = x_ref[...]
    xe_ref[HALO + tm:, :] = xn_ref[...]
    h_ref[...] = _rmsnorm(xe_ref[...], g_ref[...]).astype(BF16)
    row = lax.broadcasted_iota(jnp.int32, (tm + 2 * HALO, 1), 0)
    inside = jnp.logical_and(jnp.logical_or(row >= HALO, s0 > 0),
                             jnp.logical_or(row < HALO + tm, s0 + tm < seq))
    t = s0 + lax.broadcasted_iota(jnp.int32, (tm, 1), 0)

    def in_proj(g):
        cols = slice(g * POOL_GROUP_DIM, (g + 1) * POOL_GROUP_DIM)
        return jnp.dot(h_ref[...], win_ref[:, cols], preferred_element_type=F32)

    u_next = in_proj(0)
    for g, w in enumerate(POOL_WINDOWS):
        cols = slice(g * POOL_GROUP_DIM, (g + 1) * POOL_GROUP_DIM)
        ue_ref[:, cols] = jnp.where(inside, u_next, 0.0)
        if g + 1 < len(POOL_WINDOWS):
            u_next = in_proj(g + 1)
        total = ue_ref[HALO - w // 2:HALO - w // 2 + tm, cols]
        for k in range(-w // 2 + 1, w // 2):
            total = total + ue_ref[HALO + k:HALO + k + tm, cols]
        cnt = (jnp.minimum(t + w // 2, seq) - jnp.maximum(t - w // 2, 0)).astype(F32)
        d = total / cnt - ue_ref[HALO:HALO + tm, cols]
        z = jnp.dot(d.astype(BF16), wgrp_ref[g], preferred_element_type=F32)
        z_ref[:, cols] = z.astype(BF16)
    y = jnp.dot(z_ref[...], wout_ref[...], preferred_element_type=F32)
    o_ref[...] = x_ref[...] + y * scale_ref[...]


def _pool(x, gains, layer, w_in, w_grp, w_out, scale, j, seq):
    m = x.shape[0]
    tm = _tile(seq, 512)
    hb = tm // HALO
    last = m // HALO - 1
    return pl.pallas_call(
        functools.partial(_pool_kernel, seq=seq),
        grid=(m // tm,),
        in_specs=[
            pl.BlockSpec((HALO, D_MODEL), lambda i: (jnp.maximum(i * hb - 1, 0), 0)),
            pl.BlockSpec((tm, D_MODEL), lambda i: (i, 0)),
            pl.BlockSpec((HALO, D_MODEL), lambda i: (jnp.minimum((i + 1) * hb, last), 0)),
            _const_spec((None, None, 1, D_MODEL), (layer, 1, 0, 0)),
            _const_spec((None, D_MODEL, D_MODEL), (j, 0, 0)),
            _const_spec((None, len(POOL_WINDOWS), POOL_GROUP_DIM, POOL_GROUP_DIM), (j, 0, 0, 0)),
            _const_spec((None, D_MODEL, D_MODEL), (j, 0, 0)),
            _const_spec((None, 1, D_MODEL), (j, 0, 0)),
        ],
        out_specs=pl.BlockSpec((tm, D_MODEL), lambda i: (i, 0)),
        out_shape=jax.ShapeDtypeStruct((m, D_MODEL), F32),
        scratch_shapes=[pltpu.VMEM((tm + 2 * HALO, D_MODEL), F32),
                        pltpu.VMEM((tm + 2 * HALO, D_MODEL), BF16),
                        pltpu.VMEM((tm + 2 * HALO, D_MODEL), F32),
                        pltpu.VMEM((tm, D_MODEL), BF16)],
        compiler_params=_params(1),
        name="pool",
    )(x, x, x, gains, w_in, w_grp, w_out, scale)


def _rope_tables(seq):
    rows = seq // GRID_W
    row = jnp.repeat(jnp.arange(rows, dtype=F32), GRID_W)
    col = jnp.tile(jnp.arange(GRID_W, dtype=F32), rows)
    n_freq = HEAD_DIM // 4
    freqs = ROPE_THETA ** (-jnp.arange(n_freq, dtype=F32) / n_freq)
    ang = jnp.concatenate([row[:, None] * freqs, col[:, None] * freqs], axis=-1)
    lane = jnp.arange(LANES)
    pair = (lane % HEAD_DIM) // 2
    cos = jnp.cos(ang)[:, pair]
    sin = jnp.sin(ang)[:, pair] * jnp.where(lane % 2 == 0, -1.0, 1.0).astype(F32)
    return cos, sin


def _qkv_weight(w):
    nq = N_Q_HEADS * HEAD_DIM
    nk = N_KV_HEADS * HEAD_DIM
    wk = w[:, nq:nq + nk].reshape(D_MODEL, N_KV_HEADS, HEAD_DIM)
    wk = jnp.pad(wk, ((0, 0), (0, 0), (0, KV_LANES - HEAD_DIM))).reshape(D_MODEL, N_KV_HEADS * KV_LANES)
    return jnp.concatenate([w[:, :nq], wk, w[:, nq + nk:]], axis=1).astype(BF16)


def _run_trunk(x, p):
    batch, seq, _ = x.shape
    x = x.reshape(batch * seq, D_MODEL)
    cos, sin = _rope_tables(seq)
    for i in range(DEPTH):
        x = _ffn(x, p["norm"], p["wg"], p["wu"], p["wd"], i, 0, 0)
        j = i // 2
        if i % 2 == 0:
            qt, k, vt = _qkv(x, p["norm"], i, p["wqkv"][j], p["qg"][j], p["kg"][j], cos, sin, seq)
            o = _attention(qt, k, vt, batch, seq)
            x = _oproj(x, o, p["wo"], j)
        else:
            x = _pool(x, p["norm"], i, p["pin"], p["pgrp"], p["pout"], p["pscale"], j, seq)
        x = _ffn(x, p["norm"], p["wg"], p["wu"], p["wd"], i, 1, 2)
    return x.reshape(batch, seq, D_MODEL)


def _prepare(norm_gains, ffn_w_gate, ffn_w_up, ffn_w_down, attn_w_qkv, attn_q_gain, attn_k_gain,
             attn_w_o, pool_w_in, pool_w_group, pool_w_out, pool_scale):
    n_attn = attn_w_qkv.shape[0]
    return {
        "norm": norm_gains.reshape(DEPTH, 3, 1, D_MODEL),
        "wg": ffn_w_gate.astype(BF16),
        "wu": ffn_w_up.astype(BF16),
        "wd": ffn_w_down.astype(BF16),
        "wqkv": [_qkv_weight(attn_w_qkv[j]) for j in range(n_attn)],
        "qg": [jnp.tile(attn_q_gain[j], LANES // HEAD_DIM).reshape(1, LANES) for j in range(n_attn)],
        "kg": [jnp.tile(attn_k_gain[j], LANES // HEAD_DIM).reshape(1, LANES) for j in range(n_attn)],
        "wo": attn_w_o.astype(BF16),
        "pin": pool_w_in.astype(BF16),
        "pgrp": pool_w_group.astype(BF16),
        "pout": pool_w_out.astype(BF16),
        "pscale": pool_scale.reshape(-1, 1, D_MODEL),
    }


def kernel(x_prompt, x_sample, norm_gains, ffn_w_gate, ffn_w_up, ffn_w_down, attn_w_qkv, attn_q_gain, attn_k_gain, attn_w_o, pool_w_in, pool_w_group, pool_w_out, pool_scale):
    p = _prepare(norm_gains, ffn_w_gate, ffn_w_up, ffn_w_down, attn_w_qkv, attn_q_gain, attn_k_gain,
                 attn_w_o, pool_w_in, pool_w_group, pool_w_out, pool_scale)
    return (_run_trunk(x_prompt, p), _run_trunk(x_sample, p))
```

```python
import functools

import jax
import jax.numpy as jnp
from jax import lax
from jax.experimental import pallas as pl
from jax.experimental.pallas import tpu as pltpu

D_MODEL = 1024
DEPTH = 4
HEAD_DIM = 64
N_Q_HEADS = D_MODEL // HEAD_DIM
N_KV_HEADS = N_Q_HEADS // 4
Q_PER_KV = N_Q_HEADS // N_KV_HEADS
ROPE_THETA = 10000.0
GRID_W = 64
POOL_WINDOWS = (2, 4, 8, 16)
POOL_GROUP_DIM = D_MODEL // len(POOL_WINDOWS)
D_FF = ((8 * D_MODEL // 3 + 127) // 128) * 128
EPS = 1e-6
Q_SCALE = HEAD_DIM ** -0.5 * 1.4426950408889634

LANES = 128
SUBLANES = 8
BF16_ROWS = 16
VMEM_BUDGET = 56 * 1024 * 1024

KV_LANES = LANES
VT_ROWS = HEAD_DIM + BF16_ROWS
HALO = SUBLANES
FF_CHUNK = 256
FFN_TM = 1024
ATTN_TK = 256
ATTN_CHUNKS = 8
ATTN_AHEAD = 2

F32 = jnp.float32
BF16 = jnp.bfloat16


def _tile(n, pref):
    return pref if n % pref == 0 else n


def _params(n_axes):
    return pltpu.CompilerParams(
        dimension_semantics=("arbitrary",) * n_axes, vmem_limit_bytes=VMEM_BUDGET)


def _rmsnorm(x, g):
    r = lax.rsqrt(jnp.mean(x * x, axis=-1, keepdims=True) + EPS)
    return (x * r) * g


def _const_spec(shape, index):
    return pl.BlockSpec(shape, lambda *_: index, pipeline_mode=pl.Buffered(1))


def _ffn_kernel(*refs, mixer_proj):
    if mixer_proj:
        x_ref, mix_ref, wp_ref, g_ref, wg_ref, wu_ref, wd_ref, o_ref, h_ref, a_ref = refs
        o_ref[...] = x_ref[...] + jnp.dot(mix_ref[...], wp_ref[...], preferred_element_type=F32)
        res_ref = o_ref
    else:
        x_ref, g_ref, wg_ref, wu_ref, wd_ref, o_ref, h_ref, a_ref = refs
        res_ref = x_ref
    h_ref[...] = _rmsnorm(res_ref[...], g_ref[...]).astype(BF16)
    for c in range(D_FF // FF_CHUNK):
        sl = slice(c * FF_CHUNK, (c + 1) * FF_CHUNK)
        h = h_ref[...]
        gate = jnp.dot(h, wg_ref[:, sl], preferred_element_type=F32)
        up = jnp.dot(h, wu_ref[:, sl], preferred_element_type=F32)
        a_ref[:, sl] = (gate * jax.nn.sigmoid(gate) * up).astype(BF16)
    y = jnp.dot(a_ref[...], wd_ref[...], preferred_element_type=F32)
    o_ref[...] = res_ref[...] + 0.5 * y


def _ffn(x, gains, wg, wu, wd, layer, which, norm_idx, mixer=None):
    m = x.shape[0]
    tm = _tile(m, FFN_TM)
    row_spec = pl.BlockSpec((tm, D_MODEL), lambda i: (i, 0))
    operands, specs = [x], [row_spec]
    if mixer is not None:
        mix, wp, j = mixer
        operands += [mix, wp]
        specs += [row_spec, _const_spec((None, D_MODEL, D_MODEL), (j, 0, 0))]
    operands += [gains, wg, wu, wd]
    specs += [
        _const_spec((None, None, 1, D_MODEL), (layer, norm_idx, 0, 0)),
        _const_spec((None, None, D_MODEL, D_FF), (layer, which, 0, 0)),
        _const_spec((None, None, D_MODEL, D_FF), (layer, which, 0, 0)),
        _const_spec((None, None, D_FF, D_MODEL), (layer, which, 0, 0)),
    ]
    return pl.pallas_call(
        functools.partial(_ffn_kernel, mixer_proj=mixer is not None),
        grid=(m // tm,),
        in_specs=specs,
        out_specs=row_spec,
        out_shape=jax.ShapeDtypeStruct((m, D_MODEL), F32),
        scratch_shapes=[pltpu.VMEM((tm, D_MODEL), BF16), pltpu.VMEM((tm, D_FF), BF16)],
        compiler_params=_params(1),
        name="ffn",
    )(*operands)


def _heads_norm_rope_t(t, gain, cos, sin, row_even):
    normed = []
    for e in range(LANES // HEAD_DIM):
        th = t[e * HEAD_DIM:(e + 1) * HEAD_DIM, :]
        r = lax.rsqrt(jnp.sum(th * th, axis=0, keepdims=True) * (1.0 / HEAD_DIM) + EPS)
        normed.append(th * r)
    xn = jnp.concatenate(normed, axis=0) * gain
    partner = jnp.where(row_even,
                        pltpu.roll(xn, LANES - 1, 0),
                        pltpu.roll(xn, 1, 0))
    return xn * cos + partner * sin


def _qkv_kernel(x_ref, g_ref, w_ref, qg_ref, kg_ref, cos_ref, sin_ref,
                qt_ref, k_ref, vt_ref, h_ref):
    tm = x_ref.shape[0]
    h_ref[...] = _rmsnorm(x_ref[...], g_ref[...]).astype(BF16)
    row_even = lax.broadcasted_iota(jnp.int32, (LANES, tm), 0) % 2 == 0
    cos = cos_ref[...]
    sin = sin_ref[...]
    q_gain = jnp.concatenate([qg_ref[...]] * (tm // LANES), axis=1)
    k_gain = jnp.concatenate([kg_ref[...]] * (tm // LANES), axis=1)

    def project_t(tile):
        cols = slice(tile * 2 * LANES, (tile + 1) * 2 * LANES)
        y = jnp.dot(h_ref[...], w_ref[:, cols], preferred_element_type=F32)
        return y[:, :LANES].T, y[:, LANES:].T

    for tile in range(D_MODEL // (2 * LANES)):
        for e, t in enumerate(project_t(tile)):
            j = 2 * tile + e
            t = _heads_norm_rope_t(t, q_gain, cos, sin, row_even)
            qt_ref[j * LANES:(j + 1) * LANES, :] = (t * Q_SCALE).astype(BF16)
    tile0 = D_MODEL // (2 * LANES)
    zeros = jnp.zeros((KV_LANES - HEAD_DIM, tm), F32)
    for tile in range(N_KV_HEADS * HEAD_DIM // (2 * LANES)):
        for e, t in enumerate(project_t(tile0 + tile)):
            t = _heads_norm_rope_t(t, k_gain, cos, sin, row_even)
            for half in range(2):
                g = 2 * (2 * tile + e) + half
                kt = jnp.concatenate([t[half * HEAD_DIM:(half + 1) * HEAD_DIM, :], zeros], axis=0)
                k_ref[:, g * KV_LANES:(g + 1) * KV_LANES] = kt.T.astype(BF16)
    tile0 += N_KV_HEADS * HEAD_DIM // (2 * LANES)
    row = lax.broadcasted_iota(jnp.int32, (BF16_ROWS, tm), 0)
    ones_rows = jnp.where(row == 0, 1.0, 0.0).astype(BF16)
    for tile in range(N_KV_HEADS * HEAD_DIM // (2 * LANES)):
        for e, t in enumerate(project_t(tile0 + tile)):
            vt = t.astype(BF16)
            for half in range(2):
                g = 2 * (2 * tile + e) + half
                vt_ref[g * VT_ROWS:g * VT_ROWS + HEAD_DIM, :] = vt[half * HEAD_DIM:(half + 1) * HEAD_DIM, :]
                vt_ref[g * VT_ROWS + HEAD_DIM:(g + 1) * VT_ROWS, :] = ones_rows


def _qkv(x, gains, layer, w, qg, kg, cos, sin, j, seq):
    m = x.shape[0]
    tm = _tile(seq, 512)
    per_seq = seq // tm
    n_w = w.shape[2]
    return pl.pallas_call(
        _qkv_kernel,
        grid=(m // tm,),
        in_specs=[
            pl.BlockSpec((tm, D_MODEL), lambda i: (i, 0)),
            _const_spec((None, None, 1, D_MODEL), (layer, 1, 0, 0)),
            _const_spec((None, D_MODEL, n_w), (j, 0, 0)),
            _const_spec((None, LANES, LANES), (j, 0, 0)),
            _const_spec((None, LANES, LANES), (j, 0, 0)),
            pl.BlockSpec((LANES, tm), lambda i: (0, i % per_seq)),
            pl.BlockSpec((LANES, tm), lambda i: (0, i % per_seq)),
        ],
        out_specs=[
            pl.BlockSpec((D_MODEL, tm), lambda i: (0, i)),
            pl.BlockSpec((tm, N_KV_HEADS * KV_LANES), lambda i: (i, 0)),
            pl.BlockSpec((N_KV_HEADS * VT_ROWS, tm), lambda i: (0, i)),
        ],
        out_shape=[
            jax.ShapeDtypeStruct((D_MODEL, m), BF16),
            jax.ShapeDtypeStruct((m, N_KV_HEADS * KV_LANES), BF16),
            jax.ShapeDtypeStruct((N_KV_HEADS * VT_ROWS, m), BF16),
        ],
        scratch_shapes=[pltpu.VMEM((tm, D_MODEL), BF16)],
        compiler_params=_params(1),
        name="qkv",
    )(x, gains, w, qg, kg, cos, sin)


def _attn_kernel(qt_ref, k_ref, vt_ref, o_ref, m_ref, acc_ref, s_ref, *, tk, chunks_per_step):
    seq = k_ref.shape[0]
    step_keys = chunks_per_step * tk
    m_ref[...] = jnp.full(m_ref.shape, -jnp.inf, F32)
    acc_ref[...] = jnp.zeros(acc_ref.shape, F32)

    def scores(k0, h):
        kc = k_ref[pl.ds(k0, tk), 0:HEAD_DIM]
        qh = qt_ref[h * HEAD_DIM:(h + 1) * HEAD_DIM, :]
        return jnp.dot(kc, qh, preferred_element_type=F32)

    ahead = s_ref.shape[0]
    n_work = chunks_per_step * Q_PER_KV
    for a in range(ahead):
        s_ref[a] = scores((a // Q_PER_KV) * tk, a % Q_PER_KV)

    def step(c, carry):
        base = c * step_keys
        work = [(pl.multiple_of(base + u * tk, tk), h)
                for u in range(chunks_per_step) for h in range(Q_PER_KV)]
        work += [(pl.multiple_of(jnp.minimum(base + step_keys + (a // Q_PER_KV) * tk, seq - tk), tk),
                  a % Q_PER_KV) for a in range(ahead)]
        pending = [s_ref[a] for a in range(ahead)]
        for idx, (k0, h) in enumerate(work[:n_work]):
            pending.append(scores(*work[idx + ahead]))
            s = pending.pop(0)
            vc = vt_ref[:, pl.ds(k0, tk)]
            m_old = m_ref[h]
            m_new = jnp.maximum(m_old, jnp.max(s, axis=0, keepdims=True))
            p = jnp.exp2(s - m_new).astype(BF16)
            alpha = jnp.exp2(m_old - m_new)
            acc_ref[h] = alpha * acc_ref[h] + jnp.dot(vc, p, preferred_element_type=F32)
            m_ref[h] = m_new
        for a in range(ahead):
            s_ref[a] = pending[a]
        return carry

    lax.fori_loop(0, seq // step_keys, step, 0)
    for j in range(Q_PER_KV // 2):
        pair = []
        for h in (2 * j, 2 * j + 1):
            a = acc_ref[h]
            pair.append(a[0:HEAD_DIM, :] / a[HEAD_DIM:HEAD_DIM + 1, :])
        o_ref[:, j * LANES:(j + 1) * LANES] = jnp.concatenate(pair, axis=0).T.astype(BF16)


def _attention(qt, k, vt, batch, seq):
    m = batch * seq
    tq = _tile(seq, 512)
    tk = _tile(seq, ATTN_TK)
    per_seq = seq // tq
    qw = Q_PER_KV * HEAD_DIM
    chunks = ATTN_CHUNKS if seq % (ATTN_CHUNKS * tk) == 0 else 1
    return pl.pallas_call(
        functools.partial(_attn_kernel, tk=tk, chunks_per_step=chunks),
        grid=(batch, N_KV_HEADS, per_seq),
        in_specs=[
            pl.BlockSpec((qw, tq), lambda b, g, i: (g, b * per_seq + i)),
            pl.BlockSpec((seq, KV_LANES), lambda b, g, i: (b, g)),
            pl.BlockSpec((VT_ROWS, seq), lambda b, g, i: (g, b)),
        ],
        out_specs=pl.BlockSpec((tq, qw), lambda b, g, i: (b * per_seq + i, g)),
        out_shape=jax.ShapeDtypeStruct((m, D_MODEL), BF16),
        scratch_shapes=[pltpu.VMEM((Q_PER_KV, 1, tq), F32),
                        pltpu.VMEM((Q_PER_KV, VT_ROWS, tq), F32),
                        pltpu.VMEM((ATTN_AHEAD, tk, tq), F32)],
        compiler_params=_params(3),
        name="attn",
    )(qt, k, vt)


def _pool_kernel(xp_ref, x_ref, xn_ref, g_ref, win_ref, wgrp_ref, wout_ref, scale_ref,
                 o_ref, xe_ref, h_ref, ue_ref, z_ref, *, seq):
    tm = x_ref.shape[0]
    s0 = (pl.program_id(0) % (seq // tm)) * tm
    xe_ref[0:HALO, :] = xp_ref[...]
    xe_ref[HALO:HALO + tm, :] = x_ref[...]
    xe_ref[HALO + tm:, :] = xn_ref[...]
    h_ref[...] = _rmsnorm(xe_ref[...], g_ref[...]).astype(BF16)
    row = lax.broadcasted_iota(jnp.int32, (tm + 2 * HALO, 1), 0)
    inside = jnp.logical_and(jnp.logical_or(row >= HALO, s0 > 0),
                             jnp.logical_or(row < HALO + tm, s0 + tm < seq))
    t = s0 + lax.broadcasted_iota(jnp.int32, (tm, 1), 0)

    def in_proj(g):
        cols = slice(g * POOL_GROUP_DIM, (g + 1) * POOL_GROUP_DIM)
        return jnp.dot(h_ref[...], win_ref[:, cols], preferred_element_type=F32)

    u_next = in_proj(0)
    for g, w in enumerate(POOL_WINDOWS):
        cols = slice(g * POOL_GROUP_DIM, (g + 1) * POOL_GROUP_DIM)
        ue_ref[:, cols] = jnp.where(inside, u_next, 0.0)
        if g + 1 < len(POOL_WINDOWS):
            u_next = in_proj(g + 1)
        total = ue_ref[HALO - w // 2:HALO - w // 2 + tm, cols]
        for k in range(-w // 2 + 1, w // 2):
            total = total + ue_ref[HALO + k:HALO + k + tm, cols]
        cnt = (jnp.minimum(t + w // 2, seq) - jnp.maximum(t - w // 2, 0)).astype(F32)
        d = total / cnt - ue_ref[HALO:HALO + tm, cols]
        z = jnp.dot(d.astype(BF16), wgrp_ref[g], preferred_element_type=F32)
        z_ref[:, cols] = z.astype(BF16)
    y = jnp.dot(z_ref[...], wout_ref[...], preferred_element_type=F32)
    o_ref[...] = x_ref[...] + y * scale_ref[...]


def _pool(x, gains, layer, w_in, w_grp, w_out, scale, j, seq):
    m = x.shape[0]
    tm = _tile(seq, 512)
    hb = tm // HALO
    last = m // HALO - 1
    return pl.pallas_call(
        functools.partial(_pool_kernel, seq=seq),
        grid=(m // tm,),
        in_specs=[
            pl.BlockSpec((HALO, D_MODEL), lambda i: (jnp.maximum(i * hb - 1, 0), 0)),
            pl.BlockSpec((tm, D_MODEL), lambda i: (i, 0)),
            pl.BlockSpec((HALO, D_MODEL), lambda i: (jnp.minimum((i + 1) * hb, last), 0)),
            _const_spec((None, None, 1, D_MODEL), (layer, 1, 0, 0)),
            _const_spec((None, D_MODEL, D_MODEL), (j, 0, 0)),
            _const_spec((None, len(POOL_WINDOWS), POOL_GROUP_DIM, POOL_GROUP_DIM), (j, 0, 0, 0)),
            _const_spec((None, D_MODEL, D_MODEL), (j, 0, 0)),
            _const_spec((None, 1, D_MODEL), (j, 0, 0)),
        ],
        out_specs=pl.BlockSpec((tm, D_MODEL), lambda i: (i, 0)),
        out_shape=jax.ShapeDtypeStruct((m, D_MODEL), F32),
        scratch_shapes=[pltpu.VMEM((tm + 2 * HALO, D_MODEL), F32),
                        pltpu.VMEM((tm + 2 * HALO, D_MODEL), BF16),
                        pltpu.VMEM((tm + 2 * HALO, D_MODEL), F32),
                        pltpu.VMEM((tm, D_MODEL), BF16)],
        compiler_params=_params(1),
        name="pool",
    )(x, x, x, gains, w_in, w_grp, w_out, scale)


def _rope_tables(seq):
    rows = seq // GRID_W
    row = jnp.repeat(jnp.arange(rows, dtype=F32), GRID_W)
    col = jnp.tile(jnp.arange(GRID_W, dtype=F32), rows)
    n_freq = HEAD_DIM // 4
    freqs = ROPE_THETA ** (-jnp.arange(n_freq, dtype=F32) / n_freq)
    ang = jnp.concatenate([row[:, None] * freqs, col[:, None] * freqs], axis=-1)
    dim = jnp.arange(LANES)
    pair = (dim % HEAD_DIM) // 2
    cos = jnp.cos(ang)[:, pair].T
    sin = (jnp.sin(ang)[:, pair] * jnp.where(dim % 2 == 0, -1.0, 1.0).astype(F32)).T
    return cos, sin


def _head_gain_tile(gain):
    g = jnp.tile(gain, (1, LANES // HEAD_DIM))
    return jnp.broadcast_to(g[:, :, None], (gain.shape[0], LANES, LANES))


def _run_trunk(x, p):
    batch, seq, _ = x.shape
    x = x.reshape(batch * seq, D_MODEL)
    cos, sin = _rope_tables(seq)
    for i in range(DEPTH):
        x = _ffn(x, p["norm"], p["wg"], p["wu"], p["wd"], i, 0, 0)
        j = i // 2
        if i % 2 == 0:
            qt, k, vt = _qkv(x, p["norm"], i, p["wqkv"], p["qg"], p["kg"], cos, sin, j, seq)
            mixer = (_attention(qt, k, vt, batch, seq), p["wo"], j)
        else:
            x = _pool(x, p["norm"], i, p["pin"], p["pgrp"], p["pout"], p["pscale"], j, seq)
            mixer = None
        x = _ffn(x, p["norm"], p["wg"], p["wu"], p["wd"], i, 1, 2, mixer)
    return x.reshape(batch, seq, D_MODEL)


def _prepare(norm_gains, ffn_w_gate, ffn_w_up, ffn_w_down, attn_w_qkv, attn_q_gain, attn_k_gain,
             attn_w_o, pool_w_in, pool_w_group, pool_w_out, pool_scale):
    return {
        "norm": norm_gains.reshape(DEPTH, 3, 1, D_MODEL),
        "wg": ffn_w_gate.astype(BF16),
        "wu": ffn_w_up.astype(BF16),
        "wd": ffn_w_down.astype(BF16),
        "wqkv": attn_w_qkv.astype(BF16),
        "qg": _head_gain_tile(attn_q_gain),
        "kg": _head_gain_tile(attn_k_gain),
        "wo": attn_w_o.astype(BF16),
        "pin": pool_w_in.astype(BF16),
        "pgrp": pool_w_group.astype(BF16),
        "pout": pool_w_out.astype(BF16),
        "pscale": pool_scale.reshape(-1, 1, D_MODEL),
    }


def kernel(x_prompt, x_sample, norm_gains, ffn_w_gate, ffn_w_up, ffn_w_down, attn_w_qkv, attn_q_gain, attn_k_gain, attn_w_o, pool_w_in, pool_w_group, pool_w_out, pool_scale):
    p = _prepare(norm_gains, ffn_w_gate, ffn_w_up, ffn_w_down, attn_w_qkv, attn_q_gain, attn_k_gain,
                 attn_w_o, pool_w_in, pool_w_group, pool_w_out, pool_scale)
    return (_run_trunk(x_prompt, p), _run_trunk(x_sample, p))
```

```python
import functools

import jax
import jax.numpy as jnp
from jax import lax
from jax.experimental import pallas as pl
from jax.experimental.pallas import tpu as pltpu

D_MODEL = 1024
DEPTH = 4
HEAD_DIM = 64
N_Q_HEADS = D_MODEL // HEAD_DIM
N_KV_HEADS = N_Q_HEADS // 4
Q_PER_KV = N_Q_HEADS // N_KV_HEADS
ROPE_THETA = 10000.0
GRID_W = 64
POOL_WINDOWS = (2, 4, 8, 16)
POOL_GROUP_DIM = D_MODEL // len(POOL_WINDOWS)
D_FF = ((8 * D_MODEL // 3 + 127) // 128) * 128
EPS = 1e-6
Q_SCALE = HEAD_DIM ** -0.5 * 1.4426950408889634

LANES = 128
SUBLANES = 8
BF16_ROWS = 16
VMEM_BUDGET = 56 * 1024 * 1024

KV_LANES = LANES
VT_ROWS = HEAD_DIM + BF16_ROWS
HALO = SUBLANES
FF_CHUNK = 256
FFN_TM = 1024
ATTN_TK = 256
ATTN_CHUNKS = 8
ATTN_AHEAD = 2
FIXED_SHIFT_RANGE = 100.0

F32 = jnp.float32
BF16 = jnp.bfloat16


def _tile(n, pref):
    return pref if n % pref == 0 else n


def _params(n_axes):
    return pltpu.CompilerParams(
        dimension_semantics=("arbitrary",) * n_axes, vmem_limit_bytes=VMEM_BUDGET)


def _rmsnorm(x, g):
    r = lax.rsqrt(jnp.mean(x * x, axis=-1, keepdims=True) + EPS)
    return (x * r) * g


def _const_spec(shape, index):
    return pl.BlockSpec(shape, lambda *_: index, pipeline_mode=pl.Buffered(1))


def _ffn_kernel(*refs, mixer_proj):
    if mixer_proj:
        x_ref, mix_ref, wp_ref, g_ref, wg_ref, wu_ref, wd_ref, o_ref, h_ref, a_ref = refs
        o_ref[...] = x_ref[...] + jnp.dot(mix_ref[...], wp_ref[...], preferred_element_type=F32)
        res_ref = o_ref
    else:
        x_ref, g_ref, wg_ref, wu_ref, wd_ref, o_ref, h_ref, a_ref = refs
        res_ref = x_ref
    h_ref[...] = _rmsnorm(res_ref[...], g_ref[...]).astype(BF16)
    for c in range(D_FF // FF_CHUNK):
        sl = slice(c * FF_CHUNK, (c + 1) * FF_CHUNK)
        h = h_ref[...]
        gate = jnp.dot(h, wg_ref[:, sl], preferred_element_type=F32)
        up = jnp.dot(h, wu_ref[:, sl], preferred_element_type=F32)
        a_ref[:, sl] = (gate * jax.nn.sigmoid(gate) * up).astype(BF16)
    y = jnp.dot(a_ref[...], wd_ref[...], preferred_element_type=F32)
    o_ref[...] = res_ref[...] + 0.5 * y


def _ffn(x, gains, wg, wu, wd, layer, which, norm_idx, mixer=None):
    m = x.shape[0]
    tm = _tile(m, FFN_TM)
    row_spec = pl.BlockSpec((tm, D_MODEL), lambda i: (i, 0))
    operands, specs = [x], [row_spec]
    if mixer is not None:
        mix, wp, j = mixer
        operands += [mix, wp]
        specs += [row_spec, _const_spec((None, D_MODEL, D_MODEL), (j, 0, 0))]
    operands += [gains, wg, wu, wd]
    specs += [
        _const_spec((None, None, 1, D_MODEL), (layer, norm_idx, 0, 0)),
        _const_spec((None, None, D_MODEL, D_FF), (layer, which, 0, 0)),
        _const_spec((None, None, D_MODEL, D_FF), (layer, which, 0, 0)),
        _const_spec((None, None, D_FF, D_MODEL), (layer, which, 0, 0)),
    ]
    return pl.pallas_call(
        functools.partial(_ffn_kernel, mixer_proj=mixer is not None),
        grid=(m // tm,),
        in_specs=specs,
        out_specs=row_spec,
        out_shape=jax.ShapeDtypeStruct((m, D_MODEL), F32),
        scratch_shapes=[pltpu.VMEM((tm, D_MODEL), BF16), pltpu.VMEM((tm, D_FF), BF16)],
        compiler_params=_params(1),
        name="ffn",
    )(*operands)


def _heads_norm_rope_t(t, gain, cos, sin, row_even):
    normed = []
    for e in range(LANES // HEAD_DIM):
        th = t[e * HEAD_DIM:(e + 1) * HEAD_DIM, :]
        r = lax.rsqrt(jnp.sum(th * th, axis=0, keepdims=True) * (1.0 / HEAD_DIM) + EPS)
        normed.append(th * r)
    xn = jnp.concatenate(normed, axis=0) * gain
    partner = jnp.where(row_even,
                        pltpu.roll(xn, LANES - 1, 0),
                        pltpu.roll(xn, 1, 0))
    return xn * cos + partner * sin


def _qkv_kernel(x_ref, g_ref, w_ref, qg_ref, kg_ref, cos_ref, sin_ref,
                qt_ref, k_ref, vt_ref, kn_ref, h_ref):
    tm = x_ref.shape[0]
    h_ref[...] = _rmsnorm(x_ref[...], g_ref[...]).astype(BF16)
    row_even = lax.broadcasted_iota(jnp.int32, (LANES, tm), 0) % 2 == 0
    cos = cos_ref[...]
    sin = sin_ref[...]
    q_gain = jnp.concatenate([qg_ref[...]] * (tm // LANES), axis=1)
    k_gain = jnp.concatenate([kg_ref[...]] * (tm // LANES), axis=1)

    def project_t(tile):
        cols = slice(tile * 2 * LANES, (tile + 1) * 2 * LANES)
        y = jnp.dot(h_ref[...], w_ref[:, cols], preferred_element_type=F32)
        return y[:, :LANES].T, y[:, LANES:].T

    for tile in range(D_MODEL // (2 * LANES)):
        for e, t in enumerate(project_t(tile)):
            j = 2 * tile + e
            t = _heads_norm_rope_t(t, q_gain, cos, sin, row_even)
            qt_ref[j * LANES:(j + 1) * LANES, :] = (t * Q_SCALE).astype(BF16)
    tile0 = D_MODEL // (2 * LANES)
    zeros = jnp.zeros((KV_LANES - HEAD_DIM, tm), F32)
    key_norms = []
    for tile in range(N_KV_HEADS * HEAD_DIM // (2 * LANES)):
        for e, t in enumerate(project_t(tile0 + tile)):
            t = _heads_norm_rope_t(t, k_gain, cos, sin, row_even)
            for half in range(2):
                g = 2 * (2 * tile + e) + half
                kh = t[half * HEAD_DIM:(half + 1) * HEAD_DIM, :]
                k_ref[:, g * KV_LANES:(g + 1) * KV_LANES] = jnp.concatenate([kh, zeros], axis=0).T.astype(BF16)
                kb = kh.astype(BF16).astype(F32)
                key_norms.append(jnp.sum(kb * kb, axis=0, keepdims=True))
    key_norms.append(jnp.zeros((SUBLANES - N_KV_HEADS, tm), F32))
    kn_ref[...] = jnp.concatenate(key_norms, axis=0)
    tile0 += N_KV_HEADS * HEAD_DIM // (2 * LANES)
    row = lax.broadcasted_iota(jnp.int32, (BF16_ROWS, tm), 0)
    ones_rows = jnp.where(row == 0, 1.0, 0.0).astype(BF16)
    for tile in range(N_KV_HEADS * HEAD_DIM // (2 * LANES)):
        for e, t in enumerate(project_t(tile0 + tile)):
            vt = t.astype(BF16)
            for half in range(2):
                g = 2 * (2 * tile + e) + half
                vt_ref[g * VT_ROWS:g * VT_ROWS + HEAD_DIM, :] = vt[half * HEAD_DIM:(half + 1) * HEAD_DIM, :]
                vt_ref[g * VT_ROWS + HEAD_DIM:(g + 1) * VT_ROWS, :] = ones_rows


def _qkv(x, gains, layer, w, qg, kg, cos, sin, j, seq):
    m = x.shape[0]
    tm = _tile(seq, 512)
    per_seq = seq // tm
    n_w = w.shape[2]
    return pl.pallas_call(
        _qkv_kernel,
        grid=(m // tm,),
        in_specs=[
            pl.BlockSpec((tm, D_MODEL), lambda i: (i, 0)),
            _const_spec((None, None, 1, D_MODEL), (layer, 1, 0, 0)),
            _const_spec((None, D_MODEL, n_w), (j, 0, 0)),
            _const_spec((None, LANES, LANES), (j, 0, 0)),
            _const_spec((None, LANES, LANES), (j, 0, 0)),
            pl.BlockSpec((LANES, tm), lambda i: (0, i % per_seq)),
            pl.BlockSpec((LANES, tm), lambda i: (0, i % per_seq)),
        ],
        out_specs=[
            pl.BlockSpec((D_MODEL, tm), lambda i: (0, i)),
            pl.BlockSpec((tm, N_KV_HEADS * KV_LANES), lambda i: (i, 0)),
            pl.BlockSpec((N_KV_HEADS * VT_ROWS, tm), lambda i: (0, i)),
            pl.BlockSpec((SUBLANES, tm), lambda i: (0, i)),
        ],
        out_shape=[
            jax.ShapeDtypeStruct((D_MODEL, m), BF16),
            jax.ShapeDtypeStruct((m, N_KV_HEADS * KV_LANES), BF16),
            jax.ShapeDtypeStruct((N_KV_HEADS * VT_ROWS, m), BF16),
            jax.ShapeDtypeStruct((SUBLANES, m), F32),
        ],
        scratch_shapes=[pltpu.VMEM((tm, D_MODEL), BF16)],
        compiler_params=_params(1),
        name="qkv",
    )(x, gains, w, qg, kg, cos, sin)


def _attn_kernel(qt_ref, k_ref, vt_ref, kn_ref, o_ref, m_ref, acc_ref, s_ref, p_ref,
                 *, tk, chunks_per_step):
    seq = k_ref.shape[0]
    step_keys = chunks_per_step * tk
    ahead = s_ref.shape[0]
    n_work = chunks_per_step * Q_PER_KV
    acc_ref[...] = jnp.zeros(acc_ref.shape, F32)

    def scores(k0, h):
        kc = k_ref[pl.ds(k0, tk), 0:HEAD_DIM]
        qh = qt_ref[h * HEAD_DIM:(h + 1) * HEAD_DIM, :]
        return jnp.dot(kc, qh, preferred_element_type=F32)

    def pipeline(produce, consume, carry_ref):
        for a in range(ahead):
            carry_ref[a] = produce((a // Q_PER_KV) * tk, a % Q_PER_KV)

        def step(c, carry):
            base = c * step_keys
            work = [(pl.multiple_of(base + u * tk, tk), h)
                    for u in range(chunks_per_step) for h in range(Q_PER_KV)]
            work += [(pl.multiple_of(jnp.minimum(base + step_keys + (a // Q_PER_KV) * tk, seq - tk), tk),
                      a % Q_PER_KV) for a in range(ahead)]
            pending = [carry_ref[a] for a in range(ahead)]
            for idx, (k0, h) in enumerate(work[:n_work]):
                pending.append(produce(*work[idx + ahead]))
                consume(k0, h, pending.pop(0))
            for a in range(ahead):
                carry_ref[a] = pending[a]
            return carry

        lax.fori_loop(0, seq // step_keys, step, 0)

    kn2 = kn_ref[...]
    this_head = lax.broadcasted_iota(jnp.int32, kn2.shape, 0) == pl.program_id(1)
    k_max = jnp.sqrt(jnp.max(jnp.where(this_head, kn2, 0.0)))
    shift = []
    for h in range(Q_PER_KV):
        qh = qt_ref[h * HEAD_DIM:(h + 1) * HEAD_DIM, :].astype(F32)
        shift.append(jnp.sqrt(jnp.sum(qh * qh, axis=0, keepdims=True)) * k_max)
    fixed_ok = 2.0 * jnp.max(jnp.concatenate(shift, axis=0)) <= FIXED_SHIFT_RANGE

    @pl.when(fixed_ok)
    def _fixed_shift():
        for h in range(Q_PER_KV):
            m_ref[h] = shift[h]

        def probs(k0, h):
            return jnp.exp2(scores(k0, h) - m_ref[h]).astype(BF16)

        def accumulate(k0, h, p):
            acc_ref[h] += jnp.dot(vt_ref[:, pl.ds(k0, tk)], p, preferred_element_type=F32)

        pipeline(probs, accumulate, p_ref)

    @pl.when(jnp.logical_not(fixed_ok))
    def _online_max():
        m_ref[...] = jnp.full(m_ref.shape, -jnp.inf, F32)

        def update(k0, h, s):
            m_old = m_ref[h]
            m_new = jnp.maximum(m_old, jnp.max(s, axis=0, keepdims=True))
            p = jnp.exp2(s - m_new).astype(BF16)
            alpha = jnp.exp2(m_old - m_new)
            pv = jnp.dot(vt_ref[:, pl.ds(k0, tk)], p, preferred_element_type=F32)
            acc_ref[h] = alpha * acc_ref[h] + pv
            m_ref[h] = m_new

        pipeline(scores, update, s_ref)

    for j in range(Q_PER_KV // 2):
        pair = []
        for h in (2 * j, 2 * j + 1):
            a = acc_ref[h]
            pair.append(a[0:HEAD_DIM, :] / a[HEAD_DIM:HEAD_DIM + 1, :])
        o_ref[:, j * LANES:(j + 1) * LANES] = jnp.concatenate(pair, axis=0).T.astype(BF16)


def _attention(qt, k, vt, kn, batch, seq):
    m = batch * seq
    tq = _tile(seq, 512)
    tk = _tile(seq, ATTN_TK)
    per_seq = seq // tq
    qw = Q_PER_KV * HEAD_DIM
    chunks = ATTN_CHUNKS if seq % (ATTN_CHUNKS * tk) == 0 else 1
    return pl.pallas_call(
        functools.partial(_attn_kernel, tk=tk, chunks_per_step=chunks),
        grid=(batch, N_KV_HEADS, per_seq),
        in_specs=[
            pl.BlockSpec((qw, tq), lambda b, g, i: (g, b * per_seq + i)),
            pl.BlockSpec((seq, KV_LANES), lambda b, g, i: (b, g)),
            pl.BlockSpec((VT_ROWS, seq), lambda b, g, i: (g, b)),
            pl.BlockSpec((SUBLANES, seq), lambda b, g, i: (0, b)),
        ],
        out_specs=pl.BlockSpec((tq, qw), lambda b, g, i: (b * per_seq + i, g)),
        out_shape=jax.ShapeDtypeStruct((m, D_MODEL), BF16),
        scratch_shapes=[pltpu.VMEM((Q_PER_KV, 1, tq), F32),
                        pltpu.VMEM((Q_PER_KV, VT_ROWS, tq), F32),
                        pltpu.VMEM((ATTN_AHEAD, tk, tq), F32),
                        pltpu.VMEM((ATTN_AHEAD, tk, tq), BF16)],
        compiler_params=_params(3),
        name="attn",
    )(qt, k, vt, kn)


def _pool_kernel(xp_ref, x_ref, xn_ref, g_ref, win_ref, wgrp_ref, wout_ref, scale_ref,
                 o_ref, xe_ref, h_ref, ue_ref, z_ref, *, seq):
    tm = x_ref.shape[0]
    s0 = (pl.program_id(0) % (seq // tm)) * tm
    xe_ref[0:HALO, :] = xp_ref[...]
    xe_ref[HALO:HALO + tm, :] = x_ref[...]
    xe_ref[HALO + tm:, :] = xn_ref[...]
    h_ref[...] = _rmsnorm(xe_ref[...], g_ref[...]).astype(BF16)
    row = lax.broadcasted_iota(jnp.int32, (tm + 2 * HALO, 1), 0)
    inside = jnp.logical_and(jnp.logical_or(row >= HALO, s0 > 0),
                             jnp.logical_or(row < HALO + tm, s0 + tm < seq))
    t = s0 + lax.broadcasted_iota(jnp.int32, (tm, 1), 0)

    def in_proj(g):
        cols = slice(g * POOL_GROUP_DIM, (g + 1) * POOL_GROUP_DIM)
        return jnp.dot(h_ref[...], win_ref[:, cols], preferred_element_type=F32)

    u_next = in_proj(0)
    for g, w in enumerate(POOL_WINDOWS):
        cols = slice(g * POOL_GROUP_DIM, (g + 1) * POOL_GROUP_DIM)
        ue_ref[:, cols] = jnp.where(inside, u_next, 0.0)
        if g + 1 < len(POOL_WINDOWS):
            u_next = in_proj(g + 1)
        total = ue_ref[HALO - w // 2:HALO - w // 2 + tm, cols]
        for k in range(-w // 2 + 1, w // 2):
            total = total + ue_ref[HALO + k:HALO + k + tm, cols]
        cnt = (jnp.minimum(t + w // 2, seq) - jnp.maximum(t - w // 2, 0)).astype(F32)
        d = total / cnt - ue_ref[HALO:HALO + tm, cols]
        z = jnp.dot(d.astype(BF16), wgrp_ref[g], preferred_element_type=F32)
        z_ref[:, cols] = z.astype(BF16)
    y = jnp.dot(z_ref[...], wout_ref[...], preferred_element_type=F32)
    o_ref[...] = x_ref[...] + y * scale_ref[...]


def _pool(x, gains, layer, w_in, w_grp, w_out, scale, j, seq):
    m = x.shape[0]
    tm = _tile(seq, 512)
    hb = tm // HALO
    last = m // HALO - 1
    return pl.pallas_call(
        functools.partial(_pool_kernel, seq=seq),
        grid=(m // tm,),
        in_specs=[
            pl.BlockSpec((HALO, D_MODEL), lambda i: (jnp.maximum(i * hb - 1, 0), 0)),
            pl.BlockSpec((tm, D_MODEL), lambda i: (i, 0)),
            pl.BlockSpec((HALO, D_MODEL), lambda i: (jnp.minimum((i + 1) * hb, last), 0)),
            _const_spec((None, None, 1, D_MODEL), (layer, 1, 0, 0)),
            _const_spec((None, D_MODEL, D_MODEL), (j, 0, 0)),
            _const_spec((None, len(POOL_WINDOWS), POOL_GROUP_DIM, POOL_GROUP_DIM), (j, 0, 0, 0)),
            _const_spec((None, D_MODEL, D_MODEL), (j, 0, 0)),
            _const_spec((None, 1, D_MODEL), (j, 0, 0)),
        ],
        out_specs=pl.BlockSpec((tm, D_MODEL), lambda i: (i, 0)),
        out_shape=jax.ShapeDtypeStruct((m, D_MODEL), F32),
        scratch_shapes=[pltpu.VMEM((tm + 2 * HALO, D_MODEL), F32),
                        pltpu.VMEM((tm + 2 * HALO, D_MODEL), BF16),
                        pltpu.VMEM((tm + 2 * HALO, D_MODEL), F32),
                        pltpu.VMEM((tm, D_MODEL), BF16)],
        compiler_params=_params(1),
        name="pool",
    )(x, x, x, gains, w_in, w_grp, w_out, scale)


def _rope_tables(seq):
    rows = seq // GRID_W
    row = jnp.repeat(jnp.arange(rows, dtype=F32), GRID_W)
    col = jnp.tile(jnp.arange(GRID_W, dtype=F32), rows)
    n_freq = HEAD_DIM // 4
    freqs = ROPE_THETA ** (-jnp.arange(n_freq, dtype=F32) / n_freq)
    ang = jnp.concatenate([row[:, None] * freqs, col[:, None] * freqs], axis=-1)
    dim = jnp.arange(LANES)
    pair = (dim % HEAD_DIM) // 2
    cos = jnp.cos(ang)[:, pair].T
    sin = (jnp.sin(ang)[:, pair] * jnp.where(dim % 2 == 0, -1.0, 1.0).astype(F32)).T
    return cos, sin


def _head_gain_tile(gain):
    g = jnp.tile(gain, (1, LANES // HEAD_DIM))
    return jnp.broadcast_to(g[:, :, None], (gain.shape[0], LANES, LANES))


def _run_trunk(x, p):
    batch, seq, _ = x.shape
    x = x.reshape(batch * seq, D_MODEL)
    cos, sin = _rope_tables(seq)
    for i in range(DEPTH):
        x = _ffn(x, p["norm"], p["wg"], p["wu"], p["wd"], i, 0, 0)
        j = i // 2
        if i % 2 == 0:
            qt, k, vt, kn = _qkv(x, p["norm"], i, p["wqkv"], p["qg"], p["kg"], cos, sin, j, seq)
            mixer = (_attention(qt, k, vt, kn, batch, seq), p["wo"], j)
        else:
            x = _pool(x, p["norm"], i, p["pin"], p["pgrp"], p["pout"], p["pscale"], j, seq)
            mixer = None
        x = _ffn(x, p["norm"], p["wg"], p["wu"], p["wd"], i, 1, 2, mixer)
    return x.reshape(batch, seq, D_MODEL)


def _prepare(norm_gains, ffn_w_gate, ffn_w_up, ffn_w_down, attn_w_qkv, attn_q_gain, attn_k_gain,
             attn_w_o, pool_w_in, pool_w_group, pool_w_out, pool_scale):
    return {
        "norm": norm_gains.reshape(DEPTH, 3, 1, D_MODEL),
        "wg": ffn_w_gate.astype(BF16),
        "wu": ffn_w_up.astype(BF16),
        "wd": ffn_w_down.astype(BF16),
        "wqkv": attn_w_qkv.astype(BF16),
        "qg": _head_gain_tile(attn_q_gain),
        "kg": _head_gain_tile(attn_k_gain),
        "wo": attn_w_o.astype(BF16),
        "pin": pool_w_in.astype(BF16),
        "pgrp": pool_w_group.astype(BF16),
        "pout": pool_w_out.astype(BF16),
        "pscale": pool_scale.reshape(-1, 1, D_MODEL),
    }


def kernel(x_prompt, x_sample, norm_gains, ffn_w_gate, ffn_w_up, ffn_w_down, attn_w_qkv, attn_q_gain, attn_k_gain, attn_w_o, pool_w_in, pool_w_group, pool_w_out, pool_scale):
    p = _prepare(norm_gains, ffn_w_gate, ffn_w_up, ffn_w_down, attn_w_qkv, attn_q_gain, attn_k_gain,
                 attn_w_o, pool_w_in, pool_w_group, pool_w_out, pool_scale)
    return (_run_trunk(x_prompt, p), _run_trunk(x_sample, p))
```

```python
import functools

import jax
import jax.numpy as jnp
from jax import lax
from jax.experimental import pallas as pl
from jax.experimental.pallas import tpu as pltpu

D_MODEL = 1024
DEPTH = 4
HEAD_DIM = 64
N_Q_HEADS = D_MODEL // HEAD_DIM
N_KV_HEADS = N_Q_HEADS // 4
Q_PER_KV = N_Q_HEADS // N_KV_HEADS
ROPE_THETA = 10000.0
GRID_W = 64
POOL_WINDOWS = (2, 4, 8, 16)
POOL_GROUP_DIM = D_MODEL // len(POOL_WINDOWS)
D_FF = ((8 * D_MODEL // 3 + 127) // 128) * 128
EPS = 1e-6
Q_SCALE = HEAD_DIM ** -0.5 * 1.4426950408889634

LANES = 128
SUBLANES = 8
BF16_ROWS = 16
VMEM_BUDGET = 56 * 1024 * 1024

KV_LANES = LANES
VT_ROWS = HEAD_DIM + BF16_ROWS
HALO = SUBLANES
FF_CHUNK = 256
FFN_TM = 1024
ATTN_TK = 256
ATTN_CHUNKS = 8
ATTN_AHEAD = 2
SPARE_SLOT = Q_PER_KV
FIXED_SHIFT_RANGE = 100.0

F32 = jnp.float32
BF16 = jnp.bfloat16


def _tile(n, pref):
    return pref if n % pref == 0 else n


def _params(n_axes):
    return pltpu.CompilerParams(
        dimension_semantics=("arbitrary",) * n_axes, vmem_limit_bytes=VMEM_BUDGET)


def _rmsnorm(x, g):
    r = lax.rsqrt(jnp.mean(x * x, axis=-1, keepdims=True) + EPS)
    return (x * r) * g


def _const_spec(shape, index):
    return pl.BlockSpec(shape, lambda *_: index, pipeline_mode=pl.Buffered(1))


def _ffn_kernel(*refs, mixer_proj):
    if mixer_proj:
        x_ref, mix_ref, wp_ref, g_ref, wg_ref, wu_ref, wd_ref, o_ref, h_ref, a_ref = refs
        o_ref[...] = x_ref[...] + jnp.dot(mix_ref[...], wp_ref[...], preferred_element_type=F32)
        res_ref = o_ref
    else:
        x_ref, g_ref, wg_ref, wu_ref, wd_ref, o_ref, h_ref, a_ref = refs
        res_ref = x_ref
    h_ref[...] = _rmsnorm(res_ref[...], g_ref[...]).astype(BF16)
    for c in range(D_FF // FF_CHUNK):
        sl = slice(c * FF_CHUNK, (c + 1) * FF_CHUNK)
        h = h_ref[...]
        gate = jnp.dot(h, wg_ref[:, sl], preferred_element_type=F32)
        up = jnp.dot(h, wu_ref[:, sl], preferred_element_type=F32)
        a_ref[:, sl] = (gate * jax.nn.sigmoid(gate) * up).astype(BF16)
    y = jnp.dot(a_ref[...], wd_ref[...], preferred_element_type=F32)
    o_ref[...] = res_ref[...] + 0.5 * y


def _ffn(x, gains, wg, wu, wd, layer, which, norm_idx, mixer=None):
    m = x.shape[0]
    tm = _tile(m, FFN_TM)
    row_spec = pl.BlockSpec((tm, D_MODEL), lambda i: (i, 0))
    operands, specs = [x], [row_spec]
    if mixer is not None:
        mix, wp, j = mixer
        operands += [mix, wp]
        specs += [row_spec, _const_spec((None, D_MODEL, D_MODEL), (j, 0, 0))]
    operands += [gains, wg, wu, wd]
    specs += [
        _const_spec((None, None, 1, D_MODEL), (layer, norm_idx, 0, 0)),
        _const_spec((None, None, D_MODEL, D_FF), (layer, which, 0, 0)),
        _const_spec((None, None, D_MODEL, D_FF), (layer, which, 0, 0)),
        _const_spec((None, None, D_FF, D_MODEL), (layer, which, 0, 0)),
    ]
    return pl.pallas_call(
        functools.partial(_ffn_kernel, mixer_proj=mixer is not None),
        grid=(m // tm,),
        in_specs=specs,
        out_specs=row_spec,
        out_shape=jax.ShapeDtypeStruct((m, D_MODEL), F32),
        scratch_shapes=[pltpu.VMEM((tm, D_MODEL), BF16), pltpu.VMEM((tm, D_FF), BF16)],
        compiler_params=_params(1),
        name="ffn",
    )(*operands)


def _heads_norm_rope_t(t, gain, cos, sin, row_even):
    normed = []
    for e in range(LANES // HEAD_DIM):
        th = t[e * HEAD_DIM:(e + 1) * HEAD_DIM, :]
        r = lax.rsqrt(jnp.sum(th * th, axis=0, keepdims=True) * (1.0 / HEAD_DIM) + EPS)
        normed.append(th * r)
    xn = jnp.concatenate(normed, axis=0) * gain
    partner = jnp.where(row_even,
                        pltpu.roll(xn, LANES - 1, 0),
                        pltpu.roll(xn, 1, 0))
    return xn * cos + partner * sin


def _qkv_kernel(x_ref, g_ref, w_ref, qg_ref, kg_ref, cos_ref, sin_ref,
                qt_ref, k_ref, vt_ref, kn_ref, h_ref):
    tm = x_ref.shape[0]
    h_ref[...] = _rmsnorm(x_ref[...], g_ref[...]).astype(BF16)
    row_even = lax.broadcasted_iota(jnp.int32, (LANES, tm), 0) % 2 == 0
    cos = cos_ref[...]
    sin = sin_ref[...]
    q_gain = jnp.concatenate([qg_ref[...]] * (tm // LANES), axis=1)
    k_gain = jnp.concatenate([kg_ref[...]] * (tm // LANES), axis=1)

    def project_t(tile):
        cols = slice(tile * 2 * LANES, (tile + 1) * 2 * LANES)
        y = jnp.dot(h_ref[...], w_ref[:, cols], preferred_element_type=F32)
        return y[:, :LANES].T, y[:, LANES:].T

    for tile in range(D_MODEL // (2 * LANES)):
        for e, t in enumerate(project_t(tile)):
            j = 2 * tile + e
            t = _heads_norm_rope_t(t, q_gain, cos, sin, row_even)
            qt_ref[j * LANES:(j + 1) * LANES, :] = t.astype(BF16)
    tile0 = D_MODEL // (2 * LANES)
    zeros = jnp.zeros((KV_LANES - HEAD_DIM, tm), F32)
    key_norms = []
    for tile in range(N_KV_HEADS * HEAD_DIM // (2 * LANES)):
        for e, t in enumerate(project_t(tile0 + tile)):
            t = _heads_norm_rope_t(t, k_gain, cos, sin, row_even)
            for half in range(2):
                g = 2 * (2 * tile + e) + half
                kh = t[half * HEAD_DIM:(half + 1) * HEAD_DIM, :]
                k_ref[:, g * KV_LANES:(g + 1) * KV_LANES] = jnp.concatenate([kh, zeros], axis=0).T.astype(BF16)
                kb = kh.astype(BF16).astype(F32)
                key_norms.append(jnp.sum(kb * kb, axis=0, keepdims=True))
    key_norms.append(jnp.zeros((SUBLANES - N_KV_HEADS, tm), F32))
    kn_ref[...] = jnp.concatenate(key_norms, axis=0)
    tile0 += N_KV_HEADS * HEAD_DIM // (2 * LANES)
    row = lax.broadcasted_iota(jnp.int32, (BF16_ROWS, tm), 0)
    ones_rows = jnp.where(row == 0, 1.0, 0.0).astype(BF16)
    for tile in range(N_KV_HEADS * HEAD_DIM // (2 * LANES)):
        for e, t in enumerate(project_t(tile0 + tile)):
            vt = t.astype(BF16)
            for half in range(2):
                g = 2 * (2 * tile + e) + half
                vt_ref[g * VT_ROWS:g * VT_ROWS + HEAD_DIM, :] = vt[half * HEAD_DIM:(half + 1) * HEAD_DIM, :]
                vt_ref[g * VT_ROWS + HEAD_DIM:(g + 1) * VT_ROWS, :] = ones_rows


def _qkv(x, gains, layer, w, qg, kg, cos, sin, j, seq):
    m = x.shape[0]
    tm = _tile(seq, 512)
    per_seq = seq // tm
    n_w = w.shape[2]
    return pl.pallas_call(
        _qkv_kernel,
        grid=(m // tm,),
        in_specs=[
            pl.BlockSpec((tm, D_MODEL), lambda i: (i, 0)),
            _const_spec((None, None, 1, D_MODEL), (layer, 1, 0, 0)),
            _const_spec((None, D_MODEL, n_w), (j, 0, 0)),
            _const_spec((None, LANES, LANES), (j, 0, 0)),
            _const_spec((None, LANES, LANES), (j, 0, 0)),
            pl.BlockSpec((LANES, tm), lambda i: (0, i % per_seq)),
            pl.BlockSpec((LANES, tm), lambda i: (0, i % per_seq)),
        ],
        out_specs=[
            pl.BlockSpec((D_MODEL, tm), lambda i: (0, i)),
            pl.BlockSpec((tm, N_KV_HEADS * KV_LANES), lambda i: (i, 0)),
            pl.BlockSpec((N_KV_HEADS * VT_ROWS, tm), lambda i: (0, i)),
            pl.BlockSpec((SUBLANES, tm), lambda i: (0, i)),
        ],
        out_shape=[
            jax.ShapeDtypeStruct((D_MODEL, m), BF16),
            jax.ShapeDtypeStruct((m, N_KV_HEADS * KV_LANES), BF16),
            jax.ShapeDtypeStruct((N_KV_HEADS * VT_ROWS, m), BF16),
            jax.ShapeDtypeStruct((SUBLANES, m), F32),
        ],
        scratch_shapes=[pltpu.VMEM((tm, D_MODEL), BF16)],
        compiler_params=_params(1),
        name="qkv",
    )(x, gains, w, qg, kg, cos, sin)


def _attn_kernel(qt_ref, k_ref, vt_ref, kn_ref, o_ref, m_ref, acc_ref, s_ref, p_ref, l_ref,
                 *, tk, chunks_per_step):
    seq = k_ref.shape[0]
    step_keys = chunks_per_step * tk
    n_steps = seq // step_keys
    ahead = s_ref.shape[0]
    n_work = chunks_per_step * Q_PER_KV
    acc_ref[...] = jnp.zeros(acc_ref.shape, F32)

    def scores(k0, h, slot):
        del slot
        kc = k_ref[pl.ds(k0, tk), 0:HEAD_DIM]
        qh = qt_ref[h * HEAD_DIM:(h + 1) * HEAD_DIM, :]
        return jnp.dot(kc, qh, preferred_element_type=F32)

    first = [((a // Q_PER_KV) * tk, a % Q_PER_KV) for a in range(ahead)]

    def pipeline(produce, consume, carry_ref):
        def step(c, carry):
            base = c * step_keys
            work = [(pl.multiple_of(base + u * tk, tk), h, h)
                    for u in range(chunks_per_step) for h in range(Q_PER_KV)]
            past_end = c == n_steps - 1
            work += [(pl.multiple_of(jnp.minimum(base + step_keys + k0, seq - tk), tk), h,
                      jnp.where(past_end, SPARE_SLOT, h)) for k0, h in first]
            pending = [carry_ref[a] for a in range(ahead)]
            for idx, (k0, h, _) in enumerate(work[:n_work]):
                pending.append(produce(*work[idx + ahead]))
                consume(k0, h, pending.pop(0))
            for a in range(ahead):
                carry_ref[a] = pending[a]
            return carry

        lax.fori_loop(0, n_steps, step, 0)

    for a, (k0, h) in enumerate(first):
        s_ref[a] = scores(k0, h, h)

    kn2 = kn_ref[...]
    this_head = lax.broadcasted_iota(jnp.int32, kn2.shape, 0) == pl.program_id(1)
    k_max2 = jnp.max(jnp.where(this_head, kn2, 0.0))
    qn2 = []
    for h in range(Q_PER_KV):
        qh = qt_ref[h * HEAD_DIM:(h + 1) * HEAD_DIM, :].astype(F32)
        qn2.append(jnp.sum(qh * qh, axis=0, keepdims=True))
    fixed_ok = 4.0 * k_max2 * jnp.max(jnp.concatenate(qn2, axis=0)) <= FIXED_SHIFT_RANGE ** 2

    @pl.when(fixed_ok)
    def _fixed_shift():
        for h in range(Q_PER_KV):
            m_ref[h] = jnp.sqrt(qn2[h] * k_max2)

        l_ref[...] = jnp.zeros(l_ref.shape, F32)

        def probs_of(s, h, slot):
            p = jnp.exp2(s - m_ref[h])
            l_ref[slot] += p.reshape(tk // SUBLANES, SUBLANES, p.shape[1]).sum(axis=0)
            return p.astype(BF16)

        def probs(k0, h, slot):
            return probs_of(scores(k0, h, slot), h, slot)

        def accumulate(k0, h, p):
            acc_ref[h, 0:HEAD_DIM, :] += jnp.dot(vt_ref[0:HEAD_DIM, pl.ds(k0, tk)], p,
                                                 preferred_element_type=F32)

        for a, (_, h) in enumerate(first):
            p_ref[a] = probs_of(s_ref[a], h, h)
        pipeline(probs, accumulate, p_ref)
        for h in range(Q_PER_KV):
            acc_ref[h, HEAD_DIM:HEAD_DIM + 1, :] = jnp.sum(l_ref[h], axis=0, keepdims=True)

    @pl.when(jnp.logical_not(fixed_ok))
    def _online_max():
        m_ref[...] = jnp.full(m_ref.shape, -jnp.inf, F32)

        def update(k0, h, s):
            m_old = m_ref[h]
            m_new = jnp.maximum(m_old, jnp.max(s, axis=0, keepdims=True))
            p = jnp.exp2(s - m_new).astype(BF16)
            alpha = jnp.exp2(m_old - m_new)
            pv = jnp.dot(vt_ref[:, pl.ds(k0, tk)], p, preferred_element_type=F32)
            acc_ref[h] = alpha * acc_ref[h] + pv
            m_ref[h] = m_new

        pipeline(scores, update, s_ref)

    for j in range(Q_PER_KV // 2):
        pair = []
        for h in (2 * j, 2 * j + 1):
            a = acc_ref[h]
            pair.append(a[0:HEAD_DIM, :] / a[HEAD_DIM:HEAD_DIM + 1, :])
        o_ref[:, j * LANES:(j + 1) * LANES] = jnp.concatenate(pair, axis=0).T.astype(BF16)


def _attention(qt, k, vt, kn, batch, seq):
    m = batch * seq
    tq = _tile(seq, 512)
    tk = _tile(seq, ATTN_TK)
    per_seq = seq // tq
    qw = Q_PER_KV * HEAD_DIM
    chunks = ATTN_CHUNKS if seq % (ATTN_CHUNKS * tk) == 0 else 1
    return pl.pallas_call(
        functools.partial(_attn_kernel, tk=tk, chunks_per_step=chunks),
        grid=(batch, N_KV_HEADS, per_seq),
        in_specs=[
            pl.BlockSpec((qw, tq), lambda b, g, i: (g, b * per_seq + i)),
            pl.BlockSpec((seq, KV_LANES), lambda b, g, i: (b, g)),
            pl.BlockSpec((VT_ROWS, seq), lambda b, g, i: (g, b)),
            pl.BlockSpec((SUBLANES, seq), lambda b, g, i: (0, b)),
        ],
        out_specs=pl.BlockSpec((tq, qw), lambda b, g, i: (b * per_seq + i, g)),
        out_shape=jax.ShapeDtypeStruct((m, D_MODEL), BF16),
        scratch_shapes=[pltpu.VMEM((Q_PER_KV, 1, tq), F32),
                        pltpu.VMEM((Q_PER_KV, VT_ROWS, tq), F32),
                        pltpu.VMEM((ATTN_AHEAD, tk, tq), F32),
                        pltpu.VMEM((ATTN_AHEAD, tk, tq), BF16),
                        pltpu.VMEM((Q_PER_KV + 1, SUBLANES, tq), F32)],
        compiler_params=_params(3),
        name="attn",
    )(qt, k, vt, kn)


def _pool_kernel(xp_ref, x_ref, xn_ref, g_ref, win_ref, wgrp_ref, wout_ref, scale_ref,
                 o_ref, xe_ref, h_ref, z_ref, *, seq):
    tm = x_ref.shape[0]
    s0 = (pl.program_id(0) % (seq // tm)) * tm
    xe_ref[0:HALO, :] = xp_ref[...]
    xe_ref[HALO:HALO + tm, :] = x_ref[...]
    xe_ref[HALO + tm:, :] = xn_ref[...]
    h_ref[...] = _rmsnorm(xe_ref[...], g_ref[...]).astype(BF16)
    row = lax.broadcasted_iota(jnp.int32, (tm + 2 * HALO, 1), 0)
    inside = jnp.logical_and(jnp.logical_or(row >= HALO, s0 > 0),
                             jnp.logical_or(row < HALO + tm, s0 + tm < seq))
    t = s0 + lax.broadcasted_iota(jnp.int32, (tm, 1), 0)

    def in_proj(g):
        cols = slice(g * POOL_GROUP_DIM, (g + 1) * POOL_GROUP_DIM)
        return jnp.dot(h_ref[...], win_ref[:, cols], preferred_element_type=F32)

    rows = tm + 2 * HALO
    u_next = in_proj(0)
    for g, w in enumerate(POOL_WINDOWS):
        cols = slice(g * POOL_GROUP_DIM, (g + 1) * POOL_GROUP_DIM)
        ue = jnp.where(inside, u_next, 0.0)
        if g + 1 < len(POOL_WINDOWS):
            u_next = in_proj(g + 1)
        win = ue + pltpu.roll(ue, 1, 0)
        half = 1
        while 2 * half < w:
            win = pltpu.roll(win, half, 0) + pltpu.roll(win, rows - half, 0)
            half *= 2
        cnt = (jnp.minimum(t + w // 2, seq) - jnp.maximum(t - w // 2, 0)).astype(F32)
        d = win[HALO:HALO + tm, :] / cnt - ue[HALO:HALO + tm, :]
        z = jnp.dot(d.astype(BF16), wgrp_ref[g], preferred_element_type=F32)
        z_ref[:, cols] = z.astype(BF16)
    y = jnp.dot(z_ref[...], wout_ref[...], preferred_element_type=F32)
    o_ref[...] = x_ref[...] + y * scale_ref[...]


def _pool(x, gains, layer, w_in, w_grp, w_out, scale, j, seq):
    m = x.shape[0]
    tm = _tile(seq, 512)
    hb = tm // HALO
    last = m // HALO - 1
    return pl.pallas_call(
        functools.partial(_pool_kernel, seq=seq),
        grid=(m // tm,),
        in_specs=[
            pl.BlockSpec((HALO, D_MODEL), lambda i: (jnp.maximum(i * hb - 1, 0), 0)),
            pl.BlockSpec((tm, D_MODEL), lambda i: (i, 0)),
            pl.BlockSpec((HALO, D_MODEL), lambda i: (jnp.minimum((i + 1) * hb, last), 0)),
            _const_spec((None, None, 1, D_MODEL), (layer, 1, 0, 0)),
            _const_spec((None, D_MODEL, D_MODEL), (j, 0, 0)),
            _const_spec((None, len(POOL_WINDOWS), POOL_GROUP_DIM, POOL_GROUP_DIM), (j, 0, 0, 0)),
            _const_spec((None, D_MODEL, D_MODEL), (j, 0, 0)),
            _const_spec((None, 1, D_MODEL), (j, 0, 0)),
        ],
        out_specs=pl.BlockSpec((tm, D_MODEL), lambda i: (i, 0)),
        out_shape=jax.ShapeDtypeStruct((m, D_MODEL), F32),
        scratch_shapes=[pltpu.VMEM((tm + 2 * HALO, D_MODEL), F32),
                        pltpu.VMEM((tm + 2 * HALO, D_MODEL), BF16),
                        pltpu.VMEM((tm, D_MODEL), BF16)],
        compiler_params=_params(1),
        name="pool",
    )(x, x, x, gains, w_in, w_grp, w_out, scale)


def _rope_tables(seq):
    rows = seq // GRID_W
    row = jnp.repeat(jnp.arange(rows, dtype=F32), GRID_W)
    col = jnp.tile(jnp.arange(GRID_W, dtype=F32), rows)
    n_freq = HEAD_DIM // 4
    freqs = ROPE_THETA ** (-jnp.arange(n_freq, dtype=F32) / n_freq)
    ang = jnp.concatenate([row[:, None] * freqs, col[:, None] * freqs], axis=-1)
    dim = jnp.arange(LANES)
    pair = (dim % HEAD_DIM) // 2
    cos = jnp.cos(ang)[:, pair].T
    sin = (jnp.sin(ang)[:, pair] * jnp.where(dim % 2 == 0, -1.0, 1.0).astype(F32)).T
    return cos, sin


def _head_gain_tile(gain):
    g = jnp.tile(gain, (1, LANES // HEAD_DIM))
    return jnp.broadcast_to(g[:, :, None], (gain.shape[0], LANES, LANES))


def _run_trunk(x, p):
    batch, seq, _ = x.shape
    x = x.reshape(batch * seq, D_MODEL)
    cos, sin = _rope_tables(seq)
    for i in range(DEPTH):
        x = _ffn(x, p["norm"], p["wg"], p["wu"], p["wd"], i, 0, 0)
        j = i // 2
        if i % 2 == 0:
            qt, k, vt, kn = _qkv(x, p["norm"], i, p["wqkv"], p["qg"], p["kg"], cos, sin, j, seq)
            mixer = (_attention(qt, k, vt, kn, batch, seq), p["wo"], j)
        else:
            x = _pool(x, p["norm"], i, p["pin"], p["pgrp"], p["pout"], p["pscale"], j, seq)
            mixer = None
        x = _ffn(x, p["norm"], p["wg"], p["wu"], p["wd"], i, 1, 2, mixer)
    return x.reshape(batch, seq, D_MODEL)


def _prepare(norm_gains, ffn_w_gate, ffn_w_up, ffn_w_down, attn_w_qkv, attn_q_gain, attn_k_gain,
             attn_w_o, pool_w_in, pool_w_group, pool_w_out, pool_scale):
    return {
        "norm": norm_gains.reshape(DEPTH, 3, 1, D_MODEL),
        "wg": ffn_w_gate.astype(BF16),
        "wu": ffn_w_up.astype(BF16),
        "wd": ffn_w_down.astype(BF16),
        "wqkv": attn_w_qkv.astype(BF16),
        "qg": _head_gain_tile(attn_q_gain * Q_SCALE),
        "kg": _head_gain_tile(attn_k_gain),
        "wo": attn_w_o.astype(BF16),
        "pin": pool_w_in.astype(BF16),
        "pgrp": pool_w_group.astype(BF16),
        "pout": pool_w_out.astype(BF16),
        "pscale": pool_scale.reshape(-1, 1, D_MODEL),
    }


def kernel(x_prompt, x_sample, norm_gains, ffn_w_gate, ffn_w_up, ffn_w_down, attn_w_qkv, attn_q_gain, attn_k_gain, attn_w_o, pool_w_in, pool_w_group, pool_w_out, pool_scale):
    p = _prepare(norm_gains, ffn_w_gate, ffn_w_up, ffn_w_down, attn_w_qkv, attn_q_gain, attn_k_gain,
                 attn_w_o, pool_w_in, pool_w_group, pool_w_out, pool_scale)
    return (_run_trunk(x_prompt, p), _run_trunk(x_sample, p))
```

```python
import functools

import jax
import jax.numpy as jnp
from jax import lax
from jax.experimental import pallas as pl
from jax.experimental.pallas import tpu as pltpu

D_MODEL = 1024
DEPTH = 4
HEAD_DIM = 64
N_Q_HEADS = D_MODEL // HEAD_DIM
N_KV_HEADS = N_Q_HEADS // 4
Q_PER_KV = N_Q_HEADS // N_KV_HEADS
ROPE_THETA = 10000.0
GRID_W = 64
POOL_WINDOWS = (2, 4, 8, 16)
POOL_GROUP_DIM = D_MODEL // len(POOL_WINDOWS)
D_FF = ((8 * D_MODEL // 3 + 127) // 128) * 128
EPS = 1e-6
Q_SCALE = HEAD_DIM ** -0.5 * 1.4426950408889634

LANES = 128
SUBLANES = 8
BF16_ROWS = 16
VMEM_BUDGET = 56 * 1024 * 1024

KV_LANES = LANES
VT_ROWS = HEAD_DIM + BF16_ROWS
HALO = SUBLANES
FF_CHUNK = 256
FFN_TM = 1024
ATTN_TK = 256
ATTN_TQ = 512
ATTN_Q_SUBTILES = 1
ATTN_UNITS_PER_STEP = 64
ATTN_AHEAD = 2
FIXED_SHIFT_RANGE = 100.0

F32 = jnp.float32
BF16 = jnp.bfloat16


def _tile(n, pref):
    return pref if n % pref == 0 else n


def _params(n_axes):
    return pltpu.CompilerParams(
        dimension_semantics=("arbitrary",) * n_axes, vmem_limit_bytes=VMEM_BUDGET)


def _rmsnorm(x, g):
    r = lax.rsqrt(jnp.mean(x * x, axis=-1, keepdims=True) + EPS)
    return (x * r) * g


def _const_spec(shape, index):
    return pl.BlockSpec(shape, lambda *_: index, pipeline_mode=pl.Buffered(1))


def _ffn_kernel(*refs, mixer_proj):
    if mixer_proj:
        x_ref, mix_ref, wp_ref, g_ref, wg_ref, wu_ref, wd_ref, o_ref, h_ref, a_ref = refs
        o_ref[...] = x_ref[...] + jnp.dot(mix_ref[...], wp_ref[...], preferred_element_type=F32)
        res_ref = o_ref
    else:
        x_ref, g_ref, wg_ref, wu_ref, wd_ref, o_ref, h_ref, a_ref = refs
        res_ref = x_ref
    h_ref[...] = _rmsnorm(res_ref[...], g_ref[...]).astype(BF16)
    for c in range(D_FF // FF_CHUNK):
        sl = slice(c * FF_CHUNK, (c + 1) * FF_CHUNK)
        h = h_ref[...]
        gate = jnp.dot(h, wg_ref[:, sl], preferred_element_type=F32)
        up = jnp.dot(h, wu_ref[:, sl], preferred_element_type=F32)
        a_ref[:, sl] = (gate * jax.nn.sigmoid(gate) * up).astype(BF16)
    y = jnp.dot(a_ref[...], wd_ref[...], preferred_element_type=F32)
    o_ref[...] = res_ref[...] + 0.5 * y


def _ffn(x, gains, wg, wu, wd, layer, which, norm_idx, mixer=None):
    m = x.shape[0]
    tm = _tile(m, FFN_TM)
    row_spec = pl.BlockSpec((tm, D_MODEL), lambda i: (i, 0))
    operands, specs = [x], [row_spec]
    if mixer is not None:
        mix, wp, j = mixer
        operands += [mix, wp]
        specs += [row_spec, _const_spec((None, D_MODEL, D_MODEL), (j, 0, 0))]
    operands += [gains, wg, wu, wd]
    specs += [
        _const_spec((None, None, 1, D_MODEL), (layer, norm_idx, 0, 0)),
        _const_spec((None, None, D_MODEL, D_FF), (layer, which, 0, 0)),
        _const_spec((None, None, D_MODEL, D_FF), (layer, which, 0, 0)),
        _const_spec((None, None, D_FF, D_MODEL), (layer, which, 0, 0)),
    ]
    return pl.pallas_call(
        functools.partial(_ffn_kernel, mixer_proj=mixer is not None),
        grid=(m // tm,),
        in_specs=specs,
        out_specs=row_spec,
        out_shape=jax.ShapeDtypeStruct((m, D_MODEL), F32),
        scratch_shapes=[pltpu.VMEM((tm, D_MODEL), BF16), pltpu.VMEM((tm, D_FF), BF16)],
        compiler_params=_params(1),
        name="ffn",
    )(*operands)


def _heads_norm_rope_t(t, gain, cos, sin, row_even):
    normed = []
    for e in range(LANES // HEAD_DIM):
        th = t[e * HEAD_DIM:(e + 1) * HEAD_DIM, :]
        r = lax.rsqrt(jnp.sum(th * th, axis=0, keepdims=True) * (1.0 / HEAD_DIM) + EPS)
        normed.append(th * r)
    xn = jnp.concatenate(normed, axis=0) * gain
    partner = jnp.where(row_even,
                        pltpu.roll(xn, LANES - 1, 0),
                        pltpu.roll(xn, 1, 0))
    return xn * cos + partner * sin


def _qkv_kernel(x_ref, g_ref, w_ref, qg_ref, kg_ref, cos_ref, sin_ref,
                qt_ref, k_ref, vt_ref, kn_ref, h_ref):
    tm = x_ref.shape[0]
    h_ref[...] = _rmsnorm(x_ref[...], g_ref[...]).astype(BF16)
    row_even = lax.broadcasted_iota(jnp.int32, (LANES, tm), 0) % 2 == 0
    cos = cos_ref[...]
    sin = sin_ref[...]
    q_gain = jnp.concatenate([qg_ref[...]] * (tm // LANES), axis=1)
    k_gain = jnp.concatenate([kg_ref[...]] * (tm // LANES), axis=1)

    def project_t(tile):
        cols = slice(tile * 2 * LANES, (tile + 1) * 2 * LANES)
        y = jnp.dot(h_ref[...], w_ref[:, cols], preferred_element_type=F32)
        return y[:, :LANES].T, y[:, LANES:].T

    for tile in range(D_MODEL // (2 * LANES)):
        for e, t in enumerate(project_t(tile)):
            j = 2 * tile + e
            t = _heads_norm_rope_t(t, q_gain, cos, sin, row_even)
            qt_ref[j * LANES:(j + 1) * LANES, :] = t.astype(BF16)
    tile0 = D_MODEL // (2 * LANES)
    zeros = jnp.zeros((KV_LANES - HEAD_DIM, tm), F32)
    key_norms = []
    for tile in range(N_KV_HEADS * HEAD_DIM // (2 * LANES)):
        for e, t in enumerate(project_t(tile0 + tile)):
            t = _heads_norm_rope_t(t, k_gain, cos, sin, row_even)
            for half in range(2):
                g = 2 * (2 * tile + e) + half
                kh = t[half * HEAD_DIM:(half + 1) * HEAD_DIM, :]
                k_ref[:, g * KV_LANES:(g + 1) * KV_LANES] = jnp.concatenate([kh, zeros], axis=0).T.astype(BF16)
                kb = kh.astype(BF16).astype(F32)
                key_norms.append(jnp.sum(kb * kb, axis=0, keepdims=True))
    key_norms.append(jnp.zeros((SUBLANES - N_KV_HEADS, tm), F32))
    kn_ref[...] = jnp.concatenate(key_norms, axis=0)
    tile0 += N_KV_HEADS * HEAD_DIM // (2 * LANES)
    row = lax.broadcasted_iota(jnp.int32, (BF16_ROWS, tm), 0)
    ones_rows = jnp.where(row == 0, 1.0, 0.0).astype(BF16)
    for tile in range(N_KV_HEADS * HEAD_DIM // (2 * LANES)):
        for e, t in enumerate(project_t(tile0 + tile)):
            vt = t.astype(BF16)
            for half in range(2):
                g = 2 * (2 * tile + e) + half
                vt_ref[g * VT_ROWS:g * VT_ROWS + HEAD_DIM, :] = vt[half * HEAD_DIM:(half + 1) * HEAD_DIM, :]
                vt_ref[g * VT_ROWS + HEAD_DIM:(g + 1) * VT_ROWS, :] = ones_rows


def _qkv(x, gains, layer, w, qg, kg, cos, sin, j, seq):
    m = x.shape[0]
    tm = _tile(seq, 512)
    per_seq = seq // tm
    n_w = w.shape[2]
    return pl.pallas_call(
        _qkv_kernel,
        grid=(m // tm,),
        in_specs=[
            pl.BlockSpec((tm, D_MODEL), lambda i: (i, 0)),
            _const_spec((None, None, 1, D_MODEL), (layer, 1, 0, 0)),
            _const_spec((None, D_MODEL, n_w), (j, 0, 0)),
            _const_spec((None, LANES, LANES), (j, 0, 0)),
            _const_spec((None, LANES, LANES), (j, 0, 0)),
            pl.BlockSpec((LANES, tm), lambda i: (0, i % per_seq)),
            pl.BlockSpec((LANES, tm), lambda i: (0, i % per_seq)),
        ],
        out_specs=[
            pl.BlockSpec((D_MODEL, tm), lambda i: (0, i)),
            pl.BlockSpec((tm, N_KV_HEADS * KV_LANES), lambda i: (i, 0)),
            pl.BlockSpec((N_KV_HEADS * VT_ROWS, tm), lambda i: (0, i)),
            pl.BlockSpec((SUBLANES, tm), lambda i: (0, i)),
        ],
        out_shape=[
            jax.ShapeDtypeStruct((D_MODEL, m), BF16),
            jax.ShapeDtypeStruct((m, N_KV_HEADS * KV_LANES), BF16),
            jax.ShapeDtypeStruct((N_KV_HEADS * VT_ROWS, m), BF16),
            jax.ShapeDtypeStruct((SUBLANES, m), F32),
        ],
        scratch_shapes=[pltpu.VMEM((tm, D_MODEL), BF16)],
        compiler_params=_params(1),
        name="qkv",
    )(x, gains, w, qg, kg, cos, sin)


def _attn_kernel(qt_ref, k_ref, vt_ref, kn_ref, o_ref, m_ref, acc_ref, s_ref, p_ref, l_ref,
                 *, tk, chunks_per_step):
    seq = k_ref.shape[0]
    step_keys = chunks_per_step * tk
    n_steps = seq // step_keys
    ahead = s_ref.shape[0]
    n_streams, _, tq = m_ref.shape
    spare_slot = n_streams
    n_work = chunks_per_step * n_streams
    acc_ref[...] = jnp.zeros(acc_ref.shape, F32)

    def q_of(v):
        h, sub = v % Q_PER_KV, v // Q_PER_KV
        return qt_ref[h * HEAD_DIM:(h + 1) * HEAD_DIM, sub * tq:(sub + 1) * tq]

    def scores(k0, v, slot):
        del slot
        kc = k_ref[pl.ds(k0, tk), 0:HEAD_DIM]
        return jnp.dot(kc, q_of(v), preferred_element_type=F32)

    first = [((a // n_streams) * tk, a % n_streams) for a in range(ahead)]

    def pipeline(produce, consume, carry_ref):
        def step(c, carry):
            base = c * step_keys
            work = [(pl.multiple_of(base + u * tk, tk), v, v)
                    for u in range(chunks_per_step) for v in range(n_streams)]
            if n_steps > 1:
                past_end = c == n_steps - 1
                work += [(pl.multiple_of(jnp.minimum(base + step_keys + k0, seq - tk), tk), v,
                          jnp.where(past_end, spare_slot, v)) for k0, v in first]
            pending = [carry_ref[a] for a in range(ahead)]
            for idx, (k0, v, _) in enumerate(work[:n_work]):
                if idx + ahead < len(work):
                    pending.append(produce(*work[idx + ahead]))
                consume(k0, v, pending.pop(0))
            for a, tile in enumerate(pending):
                carry_ref[a] = tile
            return carry

        lax.fori_loop(0, n_steps, step, 0)

    for a, (k0, v) in enumerate(first):
        s_ref[a] = scores(k0, v, v)

    kn2 = kn_ref[...]
    this_head = lax.broadcasted_iota(jnp.int32, kn2.shape, 0) == pl.program_id(1)
    k_max2 = jnp.max(jnp.where(this_head, kn2, 0.0))
    qn2 = []
    for v in range(n_streams):
        qv = q_of(v).astype(F32)
        qn2.append(jnp.sum(qv * qv, axis=0, keepdims=True))
    fixed_ok = 4.0 * k_max2 * jnp.max(jnp.concatenate(qn2, axis=0)) <= FIXED_SHIFT_RANGE ** 2

    @pl.when(fixed_ok)
    def _fixed_shift():
        for v in range(n_streams):
            m_ref[v] = jnp.sqrt(qn2[v] * k_max2)

        l_ref[...] = jnp.zeros(l_ref.shape, F32)

        def probs_of(s, v, slot):
            p = jnp.exp2(s - m_ref[v])
            l_ref[slot] += p.reshape(tk // SUBLANES, SUBLANES, tq).sum(axis=0)
            return p.astype(BF16)

        def probs(k0, v, slot):
            return probs_of(scores(k0, v, slot), v, slot)

        def accumulate(k0, v, p):
            acc_ref[v, 0:HEAD_DIM, :] += jnp.dot(vt_ref[0:HEAD_DIM, pl.ds(k0, tk)], p,
                                                 preferred_element_type=F32)

        for a, (_, v) in enumerate(first):
            p_ref[a] = probs_of(s_ref[a], v, v)
        pipeline(probs, accumulate, p_ref)
        for v in range(n_streams):
            acc_ref[v, HEAD_DIM:HEAD_DIM + 1, :] = jnp.sum(l_ref[v], axis=0, keepdims=True)

    @pl.when(jnp.logical_not(fixed_ok))
    def _online_max():
        m_ref[...] = jnp.full(m_ref.shape, -jnp.inf, F32)

        def update(k0, v, s):
            m_old = m_ref[v]
            m_new = jnp.maximum(m_old, jnp.max(s, axis=0, keepdims=True))
            p = jnp.exp2(s - m_new).astype(BF16)
            alpha = jnp.exp2(m_old - m_new)
            pv = jnp.dot(vt_ref[:, pl.ds(k0, tk)], p, preferred_element_type=F32)
            acc_ref[v] = alpha * acc_ref[v] + pv
            m_ref[v] = m_new

        pipeline(scores, update, s_ref)

    for sub in range(n_streams // Q_PER_KV):
        for j in range(Q_PER_KV // 2):
            pair = []
            for h in (2 * j, 2 * j + 1):
                a = acc_ref[sub * Q_PER_KV + h]
                pair.append(a[0:HEAD_DIM, :] / a[HEAD_DIM:HEAD_DIM + 1, :])
            o_ref[sub * tq:(sub + 1) * tq, j * LANES:(j + 1) * LANES] = (
                jnp.concatenate(pair, axis=0).T.astype(BF16))


def _attention(qt, k, vt, kn, batch, seq):
    m = batch * seq
    tq = _tile(seq, ATTN_TQ)
    subs = ATTN_Q_SUBTILES if seq % (ATTN_Q_SUBTILES * tq) == 0 else 1
    tqb = subs * tq
    n_streams = subs * Q_PER_KV
    tk = _tile(seq, ATTN_TK)
    per_seq = seq // tqb
    qw = Q_PER_KV * HEAD_DIM
    chunks = max(ATTN_UNITS_PER_STEP // n_streams, 1)
    while (seq // tk) % chunks:
        chunks -= 1
    return pl.pallas_call(
        functools.partial(_attn_kernel, tk=tk, chunks_per_step=chunks),
        grid=(batch, N_KV_HEADS, per_seq),
        in_specs=[
            pl.BlockSpec((qw, tqb), lambda b, g, i: (g, b * per_seq + i)),
            pl.BlockSpec((seq, KV_LANES), lambda b, g, i: (b, g)),
            pl.BlockSpec((VT_ROWS, seq), lambda b, g, i: (g, b)),
            pl.BlockSpec((SUBLANES, seq), lambda b, g, i: (0, b)),
        ],
        out_specs=pl.BlockSpec((tqb, qw), lambda b, g, i: (b * per_seq + i, g)),
        out_shape=jax.ShapeDtypeStruct((m, D_MODEL), BF16),
        scratch_shapes=[pltpu.VMEM((n_streams, 1, tq), F32),
                        pltpu.VMEM((n_streams, VT_ROWS, tq), F32),
                        pltpu.VMEM((ATTN_AHEAD, tk, tq), F32),
                        pltpu.VMEM((ATTN_AHEAD, tk, tq), BF16),
                        pltpu.VMEM((n_streams + 1, SUBLANES, tq), F32)],
        compiler_params=_params(3),
        name="attn",
    )(qt, k, vt, kn)


def _pool_kernel(xp_ref, x_ref, xn_ref, g_ref, win_ref, wgrp_ref, wout_ref, scale_ref,
                 o_ref, xe_ref, h_ref, z_ref, *, seq):
    tm = x_ref.shape[0]
    s0 = (pl.program_id(0) % (seq // tm)) * tm
    xe_ref[0:HALO, :] = xp_ref[...]
    xe_ref[HALO:HALO + tm, :] = x_ref[...]
    xe_ref[HALO + tm:, :] = xn_ref[...]
    h_ref[...] = _rmsnorm(xe_ref[...], g_ref[...]).astype(BF16)
    row = lax.broadcasted_iota(jnp.int32, (tm + 2 * HALO, 1), 0)
    inside = jnp.logical_and(jnp.logical_or(row >= HALO, s0 > 0),
                             jnp.logical_or(row < HALO + tm, s0 + tm < seq))
    t = s0 + lax.broadcasted_iota(jnp.int32, (tm, 1), 0)

    def in_proj(g):
        cols = slice(g * POOL_GROUP_DIM, (g + 1) * POOL_GROUP_DIM)
        return jnp.dot(h_ref[...], win_ref[:, cols], preferred_element_type=F32)

    rows = tm + 2 * HALO
    u_next = in_proj(0)
    for g, w in enumerate(POOL_WINDOWS):
        cols = slice(g * POOL_GROUP_DIM, (g + 1) * POOL_GROUP_DIM)
        ue = jnp.where(inside, u_next, 0.0)
        if g + 1 < len(POOL_WINDOWS):
            u_next = in_proj(g + 1)
        win = ue + pltpu.roll(ue, 1, 0)
        half = 1
        while 2 * half < w:
            win = pltpu.roll(win, half, 0) + pltpu.roll(win, rows - half, 0)
            half *= 2
        cnt = (jnp.minimum(t + w // 2, seq) - jnp.maximum(t - w // 2, 0)).astype(F32)
        d = win[HALO:HALO + tm, :] / cnt - ue[HALO:HALO + tm, :]
        z = jnp.dot(d.astype(BF16), wgrp_ref[g], preferred_element_type=F32)
        z_ref[:, cols] = z.astype(BF16)
    y = jnp.dot(z_ref[...], wout_ref[...], preferred_element_type=F32)
    o_ref[...] = x_ref[...] + y * scale_ref[...]


def _pool(x, gains, layer, w_in, w_grp, w_out, scale, j, seq):
    m = x.shape[0]
    tm = _tile(seq, 512)
    hb = tm // HALO
    last = m // HALO - 1
    return pl.pallas_call(
        functools.partial(_pool_kernel, seq=seq),
        grid=(m // tm,),
        in_specs=[
            pl.BlockSpec((HALO, D_MODEL), lambda i: (jnp.maximum(i * hb - 1, 0), 0)),
            pl.BlockSpec((tm, D_MODEL), lambda i: (i, 0)),
            pl.BlockSpec((HALO, D_MODEL), lambda i: (jnp.minimum((i + 1) * hb, last), 0)),
            _const_spec((None, None, 1, D_MODEL), (layer, 1, 0, 0)),
            _const_spec((None, D_MODEL, D_MODEL), (j, 0, 0)),
            _const_spec((None, len(POOL_WINDOWS), POOL_GROUP_DIM, POOL_GROUP_DIM), (j, 0, 0, 0)),
            _const_spec((None, D_MODEL, D_MODEL), (j, 0, 0)),
            _const_spec((None, 1, D_MODEL), (j, 0, 0)),
        ],
        out_specs=pl.BlockSpec((tm, D_MODEL), lambda i: (i, 0)),
        out_shape=jax.ShapeDtypeStruct((m, D_MODEL), F32),
        scratch_shapes=[pltpu.VMEM((tm + 2 * HALO, D_MODEL), F32),
                        pltpu.VMEM((tm + 2 * HALO, D_MODEL), BF16),
                        pltpu.VMEM((tm, D_MODEL), BF16)],
        compiler_params=_params(1),
        name="pool",
    )(x, x, x, gains, w_in, w_grp, w_out, scale)


def _rope_tables(seq):
    rows = seq // GRID_W
    row = jnp.repeat(jnp.arange(rows, dtype=F32), GRID_W)
    col = jnp.tile(jnp.arange(GRID_W, dtype=F32), rows)
    n_freq = HEAD_DIM // 4
    freqs = ROPE_THETA ** (-jnp.arange(n_freq, dtype=F32) / n_freq)
    ang = jnp.concatenate([row[:, None] * freqs, col[:, None] * freqs], axis=-1)
    dim = jnp.arange(LANES)
    pair = (dim % HEAD_DIM) // 2
    cos = jnp.cos(ang)[:, pair].T
    sin = (jnp.sin(ang)[:, pair] * jnp.where(dim % 2 == 0, -1.0, 1.0).astype(F32)).T
    return cos, sin


def _head_gain_tile(gain):
    g = jnp.tile(gain, (1, LANES // HEAD_DIM))
    return jnp.broadcast_to(g[:, :, None], (gain.shape[0], LANES, LANES))


def _run_trunk(x, p):
    batch, seq, _ = x.shape
    x = x.reshape(batch * seq, D_MODEL)
    cos, sin = _rope_tables(seq)
    for i in range(DEPTH):
        x = _ffn(x, p["norm"], p["wg"], p["wu"], p["wd"], i, 0, 0)
        j = i // 2
        if i % 2 == 0:
            qt, k, vt, kn = _qkv(x, p["norm"], i, p["wqkv"], p["qg"], p["kg"], cos, sin, j, seq)
            mixer = (_attention(qt, k, vt, kn, batch, seq), p["wo"], j)
        else:
            x = _pool(x, p["norm"], i, p["pin"], p["pgrp"], p["pout"], p["pscale"], j, seq)
            mixer = None
        x = _ffn(x, p["norm"], p["wg"], p["wu"], p["wd"], i, 1, 2, mixer)
    return x.reshape(batch, seq, D_MODEL)


def _prepare(norm_gains, ffn_w_gate, ffn_w_up, ffn_w_down, attn_w_qkv, attn_q_gain, attn_k_gain,
             attn_w_o, pool_w_in, pool_w_group, pool_w_out, pool_scale):
    return {
        "norm": norm_gains.reshape(DEPTH, 3, 1, D_MODEL),
        "wg": ffn_w_gate.astype(BF16),
        "wu": ffn_w_up.astype(BF16),
        "wd": ffn_w_down.astype(BF16),
        "wqkv": attn_w_qkv.astype(BF16),
        "qg": _head_gain_tile(attn_q_gain * Q_SCALE),
        "kg": _head_gain_tile(attn_k_gain),
        "wo": attn_w_o.astype(BF16),
        "pin": pool_w_in.astype(BF16),
        "pgrp": pool_w_group.astype(BF16),
        "pout": pool_w_out.astype(BF16),
        "pscale": pool_scale.reshape(-1, 1, D_MODEL),
    }


def kernel(x_prompt, x_sample, norm_gains, ffn_w_gate, ffn_w_up, ffn_w_down, attn_w_qkv, attn_q_gain, attn_k_gain, attn_w_o, pool_w_in, pool_w_group, pool_w_out, pool_scale):
    p = _prepare(norm_gains, ffn_w_gate, ffn_w_up, ffn_w_down, attn_w_qkv, attn_q_gain, attn_k_gain,
                 attn_w_o, pool_w_in, pool_w_group, pool_w_out, pool_scale)
    return (_run_trunk(x_prompt, p), _run_trunk(x_sample, p))
```

```python
import functools

import jax
import jax.numpy as jnp
from jax import lax
from jax.experimental import pallas as pl
from jax.experimental.pallas import tpu as pltpu

D_MODEL = 1024
DEPTH = 4
HEAD_DIM = 64
N_Q_HEADS = D_MODEL // HEAD_DIM
N_KV_HEADS = N_Q_HEADS // 4
Q_PER_KV = N_Q_HEADS // N_KV_HEADS
ROPE_THETA = 10000.0
GRID_W = 64
POOL_WINDOWS = (2, 4, 8, 16)
POOL_GROUP_DIM = D_MODEL // len(POOL_WINDOWS)
D_FF = ((8 * D_MODEL // 3 + 127) // 128) * 128
EPS = 1e-6
Q_SCALE = HEAD_DIM ** -0.5 * 1.4426950408889634

LANES = 128
SUBLANES = 8
BF16_ROWS = 16
VMEM_BUDGET = 56 * 1024 * 1024

KV_LANES = LANES
VT_ROWS = HEAD_DIM + BF16_ROWS
HALO = SUBLANES
FF_CHUNK = 256
FFN_TM = 1024
ATTN_TK = 256
ATTN_TQ = 512
ATTN_Q_SUBTILES = 2
ATTN_UNITS_PER_STEP = 128
ATTN_AHEAD = 2
FIXED_SHIFT_RANGE = 100.0

F32 = jnp.float32
BF16 = jnp.bfloat16


def _tile(n, pref):
    return pref if n % pref == 0 else n


def _params(n_axes):
    return pltpu.CompilerParams(
        dimension_semantics=("arbitrary",) * n_axes, vmem_limit_bytes=VMEM_BUDGET)


def _rmsnorm(x, g):
    r = lax.rsqrt(jnp.mean(x * x, axis=-1, keepdims=True) + EPS)
    return (x * r) * g


def _const_spec(shape, index):
    return pl.BlockSpec(shape, lambda *_: index, pipeline_mode=pl.Buffered(1))


def _ffn_kernel(*refs, mixer_proj):
    if mixer_proj:
        x_ref, mix_ref, wp_ref, g_ref, wg_ref, wu_ref, wd_ref, o_ref, h_ref, a_ref = refs
        o_ref[...] = x_ref[...] + jnp.dot(mix_ref[...], wp_ref[...], preferred_element_type=F32)
        res_ref = o_ref
    else:
        x_ref, g_ref, wg_ref, wu_ref, wd_ref, o_ref, h_ref, a_ref = refs
        res_ref = x_ref
    h_ref[...] = _rmsnorm(res_ref[...], g_ref[...]).astype(BF16)
    for c in range(D_FF // FF_CHUNK):
        sl = slice(c * FF_CHUNK, (c + 1) * FF_CHUNK)
        h = h_ref[...]
        gate = jnp.dot(h, wg_ref[:, sl], preferred_element_type=F32)
        up = jnp.dot(h, wu_ref[:, sl], preferred_element_type=F32)
        a_ref[:, sl] = (gate * jax.nn.sigmoid(gate) * up).astype(BF16)
    y = jnp.dot(a_ref[...], wd_ref[...], preferred_element_type=F32)
    o_ref[...] = res_ref[...] + 0.5 * y


def _ffn(x, gains, wg, wu, wd, layer, which, norm_idx, mixer=None):
    m = x.shape[0]
    tm = _tile(m, FFN_TM)
    row_spec = pl.BlockSpec((tm, D_MODEL), lambda i: (i, 0))
    operands, specs = [x], [row_spec]
    if mixer is not None:
        mix, wp, j = mixer
        operands += [mix, wp]
        specs += [row_spec, _const_spec((None, D_MODEL, D_MODEL), (j, 0, 0))]
    operands += [gains, wg, wu, wd]
    specs += [
        _const_spec((None, None, 1, D_MODEL), (layer, norm_idx, 0, 0)),
        _const_spec((None, None, D_MODEL, D_FF), (layer, which, 0, 0)),
        _const_spec((None, None, D_MODEL, D_FF), (layer, which, 0, 0)),
        _const_spec((None, None, D_FF, D_MODEL), (layer, which, 0, 0)),
    ]
    return pl.pallas_call(
        functools.partial(_ffn_kernel, mixer_proj=mixer is not None),
        grid=(m // tm,),
        in_specs=specs,
        out_specs=row_spec,
        out_shape=jax.ShapeDtypeStruct((m, D_MODEL), F32),
        scratch_shapes=[pltpu.VMEM((tm, D_MODEL), BF16), pltpu.VMEM((tm, D_FF), BF16)],
        compiler_params=_params(1),
        name="ffn",
    )(*operands)


def _heads_norm_rope_t(t, gain, cos, sin, row_even):
    normed = []
    for e in range(LANES // HEAD_DIM):
        th = t[e * HEAD_DIM:(e + 1) * HEAD_DIM, :]
        r = lax.rsqrt(jnp.sum(th * th, axis=0, keepdims=True) * (1.0 / HEAD_DIM) + EPS)
        normed.append(th * r)
    xn = jnp.concatenate(normed, axis=0) * gain
    partner = jnp.where(row_even,
                        pltpu.roll(xn, LANES - 1, 0),
                        pltpu.roll(xn, 1, 0))
    return xn * cos + partner * sin


def _qkv_kernel(x_ref, g_ref, w_ref, qg_ref, kg_ref, cos_ref, sin_ref,
                qt_ref, k_ref, vt_ref, kn_ref, h_ref):
    tm = x_ref.shape[0]
    h_ref[...] = _rmsnorm(x_ref[...], g_ref[...]).astype(BF16)
    row_even = lax.broadcasted_iota(jnp.int32, (LANES, tm), 0) % 2 == 0
    cos = cos_ref[...]
    sin = sin_ref[...]
    q_gain = jnp.concatenate([qg_ref[...]] * (tm // LANES), axis=1)
    k_gain = jnp.concatenate([kg_ref[...]] * (tm // LANES), axis=1)

    def project_t(tile):
        cols = slice(tile * 2 * LANES, (tile + 1) * 2 * LANES)
        y = jnp.dot(h_ref[...], w_ref[:, cols], preferred_element_type=F32)
        return y[:, :LANES].T, y[:, LANES:].T

    for tile in range(D_MODEL // (2 * LANES)):
        for e, t in enumerate(project_t(tile)):
            j = 2 * tile + e
            t = _heads_norm_rope_t(t, q_gain, cos, sin, row_even)
            qt_ref[j * LANES:(j + 1) * LANES, :] = t.astype(BF16)
    tile0 = D_MODEL // (2 * LANES)
    zeros = jnp.zeros((KV_LANES - HEAD_DIM, tm), F32)
    key_norms = []
    for tile in range(N_KV_HEADS * HEAD_DIM // (2 * LANES)):
        for e, t in enumerate(project_t(tile0 + tile)):
            t = _heads_norm_rope_t(t, k_gain, cos, sin, row_even)
            for half in range(2):
                g = 2 * (2 * tile + e) + half
                kh = t[half * HEAD_DIM:(half + 1) * HEAD_DIM, :]
                k_ref[:, g * KV_LANES:(g + 1) * KV_LANES] = jnp.concatenate([kh, zeros], axis=0).T.astype(BF16)
                kb = kh.astype(BF16).astype(F32)
                key_norms.append(jnp.sum(kb * kb, axis=0, keepdims=True))
    key_norms.append(jnp.zeros((SUBLANES - N_KV_HEADS, tm), F32))
    kn_ref[...] = jnp.concatenate(key_norms, axis=0)
    tile0 += N_KV_HEADS * HEAD_DIM // (2 * LANES)
    row = lax.broadcasted_iota(jnp.int32, (BF16_ROWS, tm), 0)
    ones_rows = jnp.where(row == 0, 1.0, 0.0).astype(BF16)
    for tile in range(N_KV_HEADS * HEAD_DIM // (2 * LANES)):
        for e, t in enumerate(project_t(tile0 + tile)):
            vt = t.astype(BF16)
            for half in range(2):
                g = 2 * (2 * tile + e) + half
                vt_ref[g * VT_ROWS:g * VT_ROWS + HEAD_DIM, :] = vt[half * HEAD_DIM:(half + 1) * HEAD_DIM, :]
                vt_ref[g * VT_ROWS + HEAD_DIM:(g + 1) * VT_ROWS, :] = ones_rows


def _qkv(x, gains, layer, w, qg, kg, cos, sin, j, seq):
    m = x.shape[0]
    tm = _tile(seq, 512)
    per_seq = seq // tm
    n_w = w.shape[2]
    return pl.pallas_call(
        _qkv_kernel,
        grid=(m // tm,),
        in_specs=[
            pl.BlockSpec((tm, D_MODEL), lambda i: (i, 0)),
            _const_spec((None, None, 1, D_MODEL), (layer, 1, 0, 0)),
            _const_spec((None, D_MODEL, n_w), (j, 0, 0)),
            _const_spec((None, LANES, LANES), (j, 0, 0)),
            _const_spec((None, LANES, LANES), (j, 0, 0)),
            pl.BlockSpec((LANES, tm), lambda i: (0, i % per_seq)),
            pl.BlockSpec((LANES, tm), lambda i: (0, i % per_seq)),
        ],
        out_specs=[
            pl.BlockSpec((D_MODEL, tm), lambda i: (0, i)),
            pl.BlockSpec((tm, N_KV_HEADS * KV_LANES), lambda i: (i, 0)),
            pl.BlockSpec((N_KV_HEADS * VT_ROWS, tm), lambda i: (0, i)),
            pl.BlockSpec((SUBLANES, tm), lambda i: (0, i)),
        ],
        out_shape=[
            jax.ShapeDtypeStruct((D_MODEL, m), BF16),
            jax.ShapeDtypeStruct((m, N_KV_HEADS * KV_LANES), BF16),
            jax.ShapeDtypeStruct((N_KV_HEADS * VT_ROWS, m), BF16),
            jax.ShapeDtypeStruct((SUBLANES, m), F32),
        ],
        scratch_shapes=[pltpu.VMEM((tm, D_MODEL), BF16)],
        compiler_params=_params(1),
        name="qkv",
    )(x, gains, w, qg, kg, cos, sin)


def _attn_kernel(qt_ref, k_ref, vt_ref, kn_ref, o_ref, m_ref, acc_ref, s_ref, p_ref, l_ref,
                 *, tk, chunks_per_step):
    seq = k_ref.shape[0]
    step_keys = chunks_per_step * tk
    n_steps = seq // step_keys
    ahead = s_ref.shape[0]
    n_streams, _, tq = m_ref.shape
    spare_slot = n_streams
    n_work = chunks_per_step * n_streams
    acc_ref[...] = jnp.zeros(acc_ref.shape, F32)

    def q_of(v):
        h, sub = v % Q_PER_KV, v // Q_PER_KV
        return qt_ref[h * HEAD_DIM:(h + 1) * HEAD_DIM, sub * tq:(sub + 1) * tq]

    def scores(k0, v, slot):
        del slot
        kc = k_ref[pl.ds(k0, tk), 0:HEAD_DIM]
        return jnp.dot(kc, q_of(v), preferred_element_type=F32)

    first = [((a // n_streams) * tk, a % n_streams) for a in range(ahead)]

    def pipeline(produce, consume, carry_ref):
        def step(c, carry):
            base = c * step_keys
            work = [(pl.multiple_of(base + u * tk, tk), v, v)
                    for u in range(chunks_per_step) for v in range(n_streams)]
            if n_steps > 1:
                past_end = c == n_steps - 1
                work += [(pl.multiple_of(jnp.minimum(base + step_keys + k0, seq - tk), tk), v,
                          jnp.where(past_end, spare_slot, v)) for k0, v in first]
            pending = [carry_ref[a] for a in range(ahead)]
            for idx, (k0, v, _) in enumerate(work[:n_work]):
                if idx + ahead < len(work):
                    pending.append(produce(*work[idx + ahead]))
                consume(k0, v, pending.pop(0))
            for a, tile in enumerate(pending):
                carry_ref[a] = tile
            return carry

        lax.fori_loop(0, n_steps, step, 0)

    for a, (k0, v) in enumerate(first):
        s_ref[a] = scores(k0, v, v)

    kn2 = kn_ref[...]
    this_head = lax.broadcasted_iota(jnp.int32, kn2.shape, 0) == pl.program_id(1)
    k_max2 = jnp.max(jnp.where(this_head, kn2, 0.0))
    qn2 = []
    for v in range(n_streams):
        qv = q_of(v).astype(F32)
        qn2.append(jnp.sum(qv * qv, axis=0, keepdims=True))
    fixed_ok = 4.0 * k_max2 * jnp.max(jnp.concatenate(qn2, axis=0)) <= FIXED_SHIFT_RANGE ** 2

    @pl.when(fixed_ok)
    def _fixed_shift():
        for v in range(n_streams):
            m_ref[v] = jnp.sqrt(qn2[v] * k_max2)

        l_ref[...] = jnp.zeros(l_ref.shape, F32)

        def probs_of(s, v, slot):
            p = jnp.exp2(s - m_ref[v])
            l_ref[slot] += p.reshape(tk // SUBLANES, SUBLANES, tq).sum(axis=0)
            return p.astype(BF16)

        def probs(k0, v, slot):
            return probs_of(scores(k0, v, slot), v, slot)

        def accumulate(k0, v, p):
            acc_ref[v, 0:HEAD_DIM, :] += jnp.dot(vt_ref[0:HEAD_DIM, pl.ds(k0, tk)], p,
                                                 preferred_element_type=F32)

        for a, (_, v) in enumerate(first):
            p_ref[a] = probs_of(s_ref[a], v, v)
        pipeline(probs, accumulate, p_ref)
        for v in range(n_streams):
            acc_ref[v, HEAD_DIM:HEAD_DIM + 1, :] = jnp.sum(l_ref[v], axis=0, keepdims=True)

    @pl.when(jnp.logical_not(fixed_ok))
    def _online_max():
        m_ref[...] = jnp.full(m_ref.shape, -jnp.inf, F32)

        def update(k0, v, s):
            m_old = m_ref[v]
            m_new = jnp.maximum(m_old, jnp.max(s, axis=0, keepdims=True))
            p = jnp.exp2(s - m_new).astype(BF16)
            alpha = jnp.exp2(m_old - m_new)
            pv = jnp.dot(vt_ref[:, pl.ds(k0, tk)], p, preferred_element_type=F32)
            acc_ref[v] = alpha * acc_ref[v] + pv
            m_ref[v] = m_new

        pipeline(scores, update, s_ref)

    for sub in range(n_streams // Q_PER_KV):
        for j in range(Q_PER_KV // 2):
            pair = []
            for h in (2 * j, 2 * j + 1):
                a = acc_ref[sub * Q_PER_KV + h]
                pair.append(a[0:HEAD_DIM, :] / a[HEAD_DIM:HEAD_DIM + 1, :])
            o_ref[sub * tq:(sub + 1) * tq, j * LANES:(j + 1) * LANES] = (
                jnp.concatenate(pair, axis=0).T.astype(BF16))


def _attention(qt, k, vt, kn, batch, seq):
    m = batch * seq
    tq = _tile(seq, ATTN_TQ)
    subs = ATTN_Q_SUBTILES if seq % (ATTN_Q_SUBTILES * tq) == 0 else 1
    tqb = subs * tq
    n_streams = subs * Q_PER_KV
    tk = _tile(seq, ATTN_TK)
    per_seq = seq // tqb
    qw = Q_PER_KV * HEAD_DIM
    chunks = max(ATTN_UNITS_PER_STEP // n_streams, 1)
    while (seq // tk) % chunks:
        chunks -= 1
    return pl.pallas_call(
        functools.partial(_attn_kernel, tk=tk, chunks_per_step=chunks),
        grid=(batch, N_KV_HEADS, per_seq),
        in_specs=[
            pl.BlockSpec((qw, tqb), lambda b, g, i: (g, b * per_seq + i)),
            pl.BlockSpec((seq, KV_LANES), lambda b, g, i: (b, g)),
            pl.BlockSpec((VT_ROWS, seq), lambda b, g, i: (g, b)),
            pl.BlockSpec((SUBLANES, seq), lambda b, g, i: (0, b)),
        ],
        out_specs=pl.BlockSpec((tqb, qw), lambda b, g, i: (b * per_seq + i, g)),
        out_shape=jax.ShapeDtypeStruct((m, D_MODEL), BF16),
        scratch_shapes=[pltpu.VMEM((n_streams, 1, tq), F32),
                        pltpu.VMEM((n_streams, VT_ROWS, tq), F32),
                        pltpu.VMEM((ATTN_AHEAD, tk, tq), F32),
                        pltpu.VMEM((ATTN_AHEAD, tk, tq), BF16),
                        pltpu.VMEM((n_streams + 1, SUBLANES, tq), F32)],
        compiler_params=_params(3),
        name="attn",
    )(qt, k, vt, kn)


def _pool_kernel(xp_ref, x_ref, xn_ref, g_ref, win_ref, wgrp_ref, wout_ref, scale_ref,
                 o_ref, xe_ref, h_ref, z_ref, *, seq):
    tm = x_ref.shape[0]
    s0 = (pl.program_id(0) % (seq // tm)) * tm
    xe_ref[0:HALO, :] = xp_ref[...]
    xe_ref[HALO:HALO + tm, :] = x_ref[...]
    xe_ref[HALO + tm:, :] = xn_ref[...]
    h_ref[...] = _rmsnorm(xe_ref[...], g_ref[...]).astype(BF16)
    row = lax.broadcasted_iota(jnp.int32, (tm + 2 * HALO, 1), 0)
    inside = jnp.logical_and(jnp.logical_or(row >= HALO, s0 > 0),
                             jnp.logical_or(row < HALO + tm, s0 + tm < seq))
    t = s0 + lax.broadcasted_iota(jnp.int32, (tm, 1), 0)

    def in_proj(g):
        cols = slice(g * POOL_GROUP_DIM, (g + 1) * POOL_GROUP_DIM)
        return jnp.dot(h_ref[...], win_ref[:, cols], preferred_element_type=F32)

    rows = tm + 2 * HALO
    u_next = in_proj(0)
    for g, w in enumerate(POOL_WINDOWS):
        cols = slice(g * POOL_GROUP_DIM, (g + 1) * POOL_GROUP_DIM)
        ue = jnp.where(inside, u_next, 0.0)
        if g + 1 < len(POOL_WINDOWS):
            u_next = in_proj(g + 1)
        win = ue + pltpu.roll(ue, 1, 0)
        half = 1
        while 2 * half < w:
            win = pltpu.roll(win, half, 0) + pltpu.roll(win, rows - half, 0)
            half *= 2
        cnt = (jnp.minimum(t + w // 2, seq) - jnp.maximum(t - w // 2, 0)).astype(F32)
        d = win[HALO:HALO + tm, :] / cnt - ue[HALO:HALO + tm, :]
        z = jnp.dot(d.astype(BF16), wgrp_ref[g], preferred_element_type=F32)
        z_ref[:, cols] = z.astype(BF16)
    y = jnp.dot(z_ref[...], wout_ref[...], preferred_element_type=F32)
    o_ref[...] = x_ref[...] + y * scale_ref[...]


def _pool(x, gains, layer, w_in, w_grp, w_out, scale, j, seq):
    m = x.shape[0]
    tm = _tile(seq, 512)
    hb = tm // HALO
    last = m // HALO - 1
    return pl.pallas_call(
        functools.partial(_pool_kernel, seq=seq),
        grid=(m // tm,),
        in_specs=[
            pl.BlockSpec((HALO, D_MODEL), lambda i: (jnp.maximum(i * hb - 1, 0), 0)),
            pl.BlockSpec((tm, D_MODEL), lambda i: (i, 0)),
            pl.BlockSpec((HALO, D_MODEL), lambda i: (jnp.minimum((i + 1) * hb, last), 0)),
            _const_spec((None, None, 1, D_MODEL), (layer, 1, 0, 0)),
            _const_spec((None, D_MODEL, D_MODEL), (j, 0, 0)),
            _const_spec((None, len(POOL_WINDOWS), POOL_GROUP_DIM, POOL_GROUP_DIM), (j, 0, 0, 0)),
            _const_spec((None, D_MODEL, D_MODEL), (j, 0, 0)),
            _const_spec((None, 1, D_MODEL), (j, 0, 0)),
        ],
        out_specs=pl.BlockSpec((tm, D_MODEL), lambda i: (i, 0)),
        out_shape=jax.ShapeDtypeStruct((m, D_MODEL), F32),
        scratch_shapes=[pltpu.VMEM((tm + 2 * HALO, D_MODEL), F32),
                        pltpu.VMEM((tm + 2 * HALO, D_MODEL), BF16),
                        pltpu.VMEM((tm, D_MODEL), BF16)],
        compiler_params=_params(1),
        name="pool",
    )(x, x, x, gains, w_in, w_grp, w_out, scale)


def _rope_tables(seq):
    rows = seq // GRID_W
    row = jnp.repeat(jnp.arange(rows, dtype=F32), GRID_W)
    col = jnp.tile(jnp.arange(GRID_W, dtype=F32), rows)
    n_freq = HEAD_DIM // 4
    freqs = ROPE_THETA ** (-jnp.arange(n_freq, dtype=F32) / n_freq)
    ang = jnp.concatenate([row[:, None] * freqs, col[:, None] * freqs], axis=-1)
    dim = jnp.arange(LANES)
    pair = (dim % HEAD_DIM) // 2
    cos = jnp.cos(ang)[:, pair].T
    sin = (jnp.sin(ang)[:, pair] * jnp.where(dim % 2 == 0, -1.0, 1.0).astype(F32)).T
    return cos, sin


def _head_gain_tile(gain):
    g = jnp.tile(gain, (1, LANES // HEAD_DIM))
    return jnp.broadcast_to(g[:, :, None], (gain.shape[0], LANES, LANES))


def _run_trunk(x, p):
    batch, seq, _ = x.shape
    x = x.reshape(batch * seq, D_MODEL)
    cos, sin = _rope_tables(seq)
    for i in range(DEPTH):
        x = _ffn(x, p["norm"], p["wg"], p["wu"], p["wd"], i, 0, 0)
        j = i // 2
        if i % 2 == 0:
            qt, k, vt, kn = _qkv(x, p["norm"], i, p["wqkv"], p["qg"], p["kg"], cos, sin, j, seq)
            mixer = (_attention(qt, k, vt, kn, batch, seq), p["wo"], j)
        else:
            x = _pool(x, p["norm"], i, p["pin"], p["pgrp"], p["pout"], p["pscale"], j, seq)
            mixer = None
        x = _ffn(x, p["norm"], p["wg"], p["wu"], p["wd"], i, 1, 2, mixer)
    return x.reshape(batch, seq, D_MODEL)


def _prepare(norm_gains, ffn_w_gate, ffn_w_up, ffn_w_down, attn_w_qkv, attn_q_gain, attn_k_gain,
             attn_w_o, pool_w_in, pool_w_group, pool_w_out, pool_scale):
    return {
        "norm": norm_gains.reshape(DEPTH, 3, 1, D_MODEL),
        "wg": ffn_w_gate.astype(BF16),
        "wu": ffn_w_up.astype(BF16),
        "wd": ffn_w_down.astype(BF16),
        "wqkv": attn_w_qkv.astype(BF16),
        "qg": _head_gain_tile(attn_q_gain * Q_SCALE),
        "kg": _head_gain_tile(attn_k_gain),
        "wo": attn_w_o.astype(BF16),
        "pin": pool_w_in.astype(BF16),
        "pgrp": pool_w_group.astype(BF16),
        "pout": pool_w_out.astype(BF16),
        "pscale": pool_scale.reshape(-1, 1, D_MODEL),
    }


def kernel(x_prompt, x_sample, norm_gains, ffn_w_gate, ffn_w_up, ffn_w_down, attn_w_qkv, attn_q_gain, attn_k_gain, attn_w_o, pool_w_in, pool_w_group, pool_w_out, pool_scale):
    p = _prepare(norm_gains, ffn_w_gate, ffn_w_up, ffn_w_down, attn_w_qkv, attn_q_gain, attn_k_gain,
                 attn_w_o, pool_w_in, pool_w_group, pool_w_out, pool_scale)
    return (_run_trunk(x_prompt, p), _run_trunk(x_sample, p))
```

```python
import functools

import jax
import jax.numpy as jnp
from jax import lax
from jax.experimental import pallas as pl
from jax.experimental.pallas import tpu as pltpu

D_MODEL = 1024
DEPTH = 4
HEAD_DIM = 64
N_Q_HEADS = D_MODEL // HEAD_DIM
N_KV_HEADS = N_Q_HEADS // 4
Q_PER_KV = N_Q_HEADS // N_KV_HEADS
ROPE_THETA = 10000.0
GRID_W = 64
POOL_WINDOWS = (2, 4, 8, 16)
POOL_GROUP_DIM = D_MODEL // len(POOL_WINDOWS)
D_FF = ((8 * D_MODEL // 3 + 127) // 128) * 128
EPS = 1e-6
Q_SCALE = HEAD_DIM ** -0.5 * 1.4426950408889634

LANES = 128
SUBLANES = 8
BF16_ROWS = 16
VMEM_BUDGET = 56 * 1024 * 1024

KV_LANES = LANES
VT_ROWS = HEAD_DIM + BF16_ROWS
HALO = SUBLANES
FF_CHUNK = 256
FFN_TM = 1024
ATTN_TK = 256
ATTN_TQ = 512
ATTN_Q_SUBTILES = 2
ATTN_UNITS_PER_STEP = 256
ATTN_ONLINE_UNITS_PER_STEP = 32
ATTN_AHEAD = 2
FIXED_SHIFT_RANGE = 100.0

F32 = jnp.float32
BF16 = jnp.bfloat16


def _tile(n, pref):
    return pref if n % pref == 0 else n


def _params(n_axes):
    return pltpu.CompilerParams(
        dimension_semantics=("arbitrary",) * n_axes, vmem_limit_bytes=VMEM_BUDGET)


def _rmsnorm(x, g):
    r = lax.rsqrt(jnp.mean(x * x, axis=-1, keepdims=True) + EPS)
    return (x * r) * g


def _const_spec(shape, index):
    return pl.BlockSpec(shape, lambda *_: index, pipeline_mode=pl.Buffered(1))


def _ffn_kernel(*refs, mixer_proj):
    if mixer_proj:
        x_ref, mix_ref, wp_ref, g_ref, wg_ref, wu_ref, wd_ref, o_ref, h_ref, a_ref = refs
        o_ref[...] = x_ref[...] + jnp.dot(mix_ref[...], wp_ref[...], preferred_element_type=F32)
        res_ref = o_ref
    else:
        x_ref, g_ref, wg_ref, wu_ref, wd_ref, o_ref, h_ref, a_ref = refs
        res_ref = x_ref
    h_ref[...] = _rmsnorm(res_ref[...], g_ref[...]).astype(BF16)
    for c in range(D_FF // FF_CHUNK):
        sl = slice(c * FF_CHUNK, (c + 1) * FF_CHUNK)
        h = h_ref[...]
        gate = jnp.dot(h, wg_ref[:, sl], preferred_element_type=F32)
        up = jnp.dot(h, wu_ref[:, sl], preferred_element_type=F32)
        a_ref[:, sl] = (gate * jax.nn.sigmoid(gate) * up).astype(BF16)
    y = jnp.dot(a_ref[...], wd_ref[...], preferred_element_type=F32)
    o_ref[...] = res_ref[...] + 0.5 * y


def _ffn(x, gains, wg, wu, wd, layer, which, norm_idx, mixer=None):
    m = x.shape[0]
    tm = _tile(m, FFN_TM)
    row_spec = pl.BlockSpec((tm, D_MODEL), lambda i: (i, 0))
    operands, specs = [x], [row_spec]
    if mixer is not None:
        mix, wp, j = mixer
        operands += [mix, wp]
        specs += [row_spec, _const_spec((None, D_MODEL, D_MODEL), (j, 0, 0))]
    operands += [gains, wg, wu, wd]
    specs += [
        _const_spec((None, None, 1, D_MODEL), (layer, norm_idx, 0, 0)),
        _const_spec((None, None, D_MODEL, D_FF), (layer, which, 0, 0)),
        _const_spec((None, None, D_MODEL, D_FF), (layer, which, 0, 0)),
        _const_spec((None, None, D_FF, D_MODEL), (layer, which, 0, 0)),
    ]
    return pl.pallas_call(
        functools.partial(_ffn_kernel, mixer_proj=mixer is not None),
        grid=(m // tm,),
        in_specs=specs,
        out_specs=row_spec,
        out_shape=jax.ShapeDtypeStruct((m, D_MODEL), F32),
        scratch_shapes=[pltpu.VMEM((tm, D_MODEL), BF16), pltpu.VMEM((tm, D_FF), BF16)],
        compiler_params=_params(1),
        name="ffn",
    )(*operands)


def _heads_norm_rope_t(t, gain, cos, sin, row_even):
    normed = []
    for e in range(LANES // HEAD_DIM):
        th = t[e * HEAD_DIM:(e + 1) * HEAD_DIM, :]
        r = lax.rsqrt(jnp.sum(th * th, axis=0, keepdims=True) * (1.0 / HEAD_DIM) + EPS)
        normed.append(th * r)
    xn = jnp.concatenate(normed, axis=0) * gain
    partner = jnp.where(row_even,
                        pltpu.roll(xn, LANES - 1, 0),
                        pltpu.roll(xn, 1, 0))
    return xn * cos + partner * sin


def _qkv_kernel(x_ref, g_ref, w_ref, qg_ref, kg_ref, cos_ref, sin_ref,
                qt_ref, k_ref, vt_ref, kn_ref, h_ref):
    tm = x_ref.shape[0]
    h_ref[...] = _rmsnorm(x_ref[...], g_ref[...]).astype(BF16)
    row_even = lax.broadcasted_iota(jnp.int32, (LANES, tm), 0) % 2 == 0
    cos = cos_ref[...]
    sin = sin_ref[...]
    q_gain = jnp.concatenate([qg_ref[...]] * (tm // LANES), axis=1)
    k_gain = jnp.concatenate([kg_ref[...]] * (tm // LANES), axis=1)

    def project_t(tile):
        cols = slice(tile * 2 * LANES, (tile + 1) * 2 * LANES)
        y = jnp.dot(h_ref[...], w_ref[:, cols], preferred_element_type=F32)
        return y[:, :LANES].T, y[:, LANES:].T

    for tile in range(D_MODEL // (2 * LANES)):
        for e, t in enumerate(project_t(tile)):
            j = 2 * tile + e
            t = _heads_norm_rope_t(t, q_gain, cos, sin, row_even)
            qt_ref[j * LANES:(j + 1) * LANES, :] = t.astype(BF16)
    tile0 = D_MODEL // (2 * LANES)
    zeros = jnp.zeros((KV_LANES - HEAD_DIM, tm), F32)
    key_norms = []
    for tile in range(N_KV_HEADS * HEAD_DIM // (2 * LANES)):
        for e, t in enumerate(project_t(tile0 + tile)):
            t = _heads_norm_rope_t(t, k_gain, cos, sin, row_even)
            for half in range(2):
                g = 2 * (2 * tile + e) + half
                kh = t[half * HEAD_DIM:(half + 1) * HEAD_DIM, :]
                k_ref[:, g * KV_LANES:(g + 1) * KV_LANES] = jnp.concatenate([kh, zeros], axis=0).T.astype(BF16)
                kb = kh.astype(BF16).astype(F32)
                key_norms.append(jnp.sum(kb * kb, axis=0, keepdims=True))
    key_norms.append(jnp.zeros((SUBLANES - N_KV_HEADS, tm), F32))
    kn_ref[...] = jnp.concatenate(key_norms, axis=0)
    tile0 += N_KV_HEADS * HEAD_DIM // (2 * LANES)
    row = lax.broadcasted_iota(jnp.int32, (BF16_ROWS, tm), 0)
    ones_rows = jnp.where(row == 0, 1.0, 0.0).astype(BF16)
    for tile in range(N_KV_HEADS * HEAD_DIM // (2 * LANES)):
        for e, t in enumerate(project_t(tile0 + tile)):
            vt = t.astype(BF16)
            for half in range(2):
                g = 2 * (2 * tile + e) + half
                vt_ref[g * VT_ROWS:g * VT_ROWS + HEAD_DIM, :] = vt[half * HEAD_DIM:(half + 1) * HEAD_DIM, :]
                vt_ref[g * VT_ROWS + HEAD_DIM:(g + 1) * VT_ROWS, :] = ones_rows


def _qkv(x, gains, layer, w, qg, kg, cos, sin, j, seq):
    m = x.shape[0]
    tm = _tile(seq, 512)
    per_seq = seq // tm
    n_w = w.shape[2]
    return pl.pallas_call(
        _qkv_kernel,
        grid=(m // tm,),
        in_specs=[
            pl.BlockSpec((tm, D_MODEL), lambda i: (i, 0)),
            _const_spec((None, None, 1, D_MODEL), (layer, 1, 0, 0)),
            _const_spec((None, D_MODEL, n_w), (j, 0, 0)),
            _const_spec((None, LANES, LANES), (j, 0, 0)),
            _const_spec((None, LANES, LANES), (j, 0, 0)),
            pl.BlockSpec((LANES, tm), lambda i: (0, i % per_seq)),
            pl.BlockSpec((LANES, tm), lambda i: (0, i % per_seq)),
        ],
        out_specs=[
            pl.BlockSpec((D_MODEL, tm), lambda i: (0, i)),
            pl.BlockSpec((tm, N_KV_HEADS * KV_LANES), lambda i: (i, 0)),
            pl.BlockSpec((N_KV_HEADS * VT_ROWS, tm), lambda i: (0, i)),
            pl.BlockSpec((SUBLANES, tm), lambda i: (0, i)),
        ],
        out_shape=[
            jax.ShapeDtypeStruct((D_MODEL, m), BF16),
            jax.ShapeDtypeStruct((m, N_KV_HEADS * KV_LANES), BF16),
            jax.ShapeDtypeStruct((N_KV_HEADS * VT_ROWS, m), BF16),
            jax.ShapeDtypeStruct((SUBLANES, m), F32),
        ],
        scratch_shapes=[pltpu.VMEM((tm, D_MODEL), BF16)],
        compiler_params=_params(1),
        name="qkv",
    )(x, gains, w, qg, kg, cos, sin)


def _attn_kernel(qt_ref, k_ref, vt_ref, kn_ref, o_ref, m_ref, acc_ref, s_ref, p_ref, l_ref,
                 *, tk, fixed_chunks, online_chunks):
    seq = k_ref.shape[0]
    ahead = s_ref.shape[0]
    n_streams, _, tq = m_ref.shape
    spare_slot = n_streams
    acc_ref[...] = jnp.zeros(acc_ref.shape, F32)

    def q_of(v):
        h, sub = v % Q_PER_KV, v // Q_PER_KV
        return qt_ref[h * HEAD_DIM:(h + 1) * HEAD_DIM, sub * tq:(sub + 1) * tq]

    def scores(k0, v, slot):
        del slot
        kc = k_ref[pl.ds(k0, tk), 0:HEAD_DIM]
        return jnp.dot(kc, q_of(v), preferred_element_type=F32)

    first = [((a // n_streams) * tk, a % n_streams) for a in range(ahead)]

    def pipeline(produce, consume, carry_ref, chunks_per_step):
        step_keys = chunks_per_step * tk
        n_steps = seq // step_keys
        n_work = chunks_per_step * n_streams

        def step(c, carry):
            base = c * step_keys
            work = [(pl.multiple_of(base + u * tk, tk), v, v)
                    for u in range(chunks_per_step) for v in range(n_streams)]
            if n_steps > 1:
                past_end = c == n_steps - 1
                work += [(pl.multiple_of(jnp.minimum(base + step_keys + k0, seq - tk), tk), v,
                          jnp.where(past_end, spare_slot, v)) for k0, v in first]
            pending = [carry_ref[a] for a in range(ahead)]
            for idx, (k0, v, _) in enumerate(work[:n_work]):
                if idx + ahead < len(work):
                    pending.append(produce(*work[idx + ahead]))
                consume(k0, v, pending.pop(0))
            for a, tile in enumerate(pending):
                carry_ref[a] = tile
            return carry

        lax.fori_loop(0, n_steps, step, 0)

    for a, (k0, v) in enumerate(first):
        s_ref[a] = scores(k0, v, v)

    kn2 = kn_ref[...]
    this_head = lax.broadcasted_iota(jnp.int32, kn2.shape, 0) == pl.program_id(1)
    k_max2 = jnp.max(jnp.where(this_head, kn2, 0.0))
    qn2 = []
    for v in range(n_streams):
        qv = q_of(v).astype(F32)
        qn2.append(jnp.sum(qv * qv, axis=0, keepdims=True))
    fixed_ok = 4.0 * k_max2 * jnp.max(jnp.concatenate(qn2, axis=0)) <= FIXED_SHIFT_RANGE ** 2

    @pl.when(fixed_ok)
    def _fixed_shift():
        for v in range(n_streams):
            m_ref[v] = jnp.sqrt(qn2[v] * k_max2)

        l_ref[...] = jnp.zeros(l_ref.shape, F32)

        def probs_of(s, v, slot):
            p = jnp.exp2(s - m_ref[v])
            l_ref[slot] += p.reshape(tk // SUBLANES, SUBLANES, tq).sum(axis=0)
            return p.astype(BF16)

        def probs(k0, v, slot):
            return probs_of(scores(k0, v, slot), v, slot)

        def accumulate(k0, v, p):
            acc_ref[v, 0:HEAD_DIM, :] += jnp.dot(vt_ref[0:HEAD_DIM, pl.ds(k0, tk)], p,
                                                 preferred_element_type=F32)

        for a, (_, v) in enumerate(first):
            p_ref[a] = probs_of(s_ref[a], v, v)
        pipeline(probs, accumulate, p_ref, fixed_chunks)
        for v in range(n_streams):
            acc_ref[v, HEAD_DIM:HEAD_DIM + 1, :] = jnp.sum(l_ref[v], axis=0, keepdims=True)

    @pl.when(jnp.logical_not(fixed_ok))
    def _online_max():
        m_ref[...] = jnp.full(m_ref.shape, -jnp.inf, F32)

        def update(k0, v, s):
            m_old = m_ref[v]
            m_new = jnp.maximum(m_old, jnp.max(s, axis=0, keepdims=True))
            p = jnp.exp2(s - m_new).astype(BF16)
            alpha = jnp.exp2(m_old - m_new)
            pv = jnp.dot(vt_ref[:, pl.ds(k0, tk)], p, preferred_element_type=F32)
            acc_ref[v] = alpha * acc_ref[v] + pv
            m_ref[v] = m_new

        pipeline(scores, update, s_ref, online_chunks)

    for sub in range(n_streams // Q_PER_KV):
        for j in range(Q_PER_KV // 2):
            pair = []
            for h in (2 * j, 2 * j + 1):
                a = acc_ref[sub * Q_PER_KV + h]
                pair.append(a[0:HEAD_DIM, :] / a[HEAD_DIM:HEAD_DIM + 1, :])
            o_ref[sub * tq:(sub + 1) * tq, j * LANES:(j + 1) * LANES] = (
                jnp.concatenate(pair, axis=0).T.astype(BF16))


def _attention(qt, k, vt, kn, batch, seq):
    m = batch * seq
    tq = _tile(seq, ATTN_TQ)
    subs = ATTN_Q_SUBTILES if seq % (ATTN_Q_SUBTILES * tq) == 0 else 1
    tqb = subs * tq
    n_streams = subs * Q_PER_KV
    tk = _tile(seq, ATTN_TK)
    per_seq = seq // tqb
    qw = Q_PER_KV * HEAD_DIM

    def chunks_for(units):
        chunks = max(units // n_streams, 1)
        while (seq // tk) % chunks:
            chunks -= 1
        return chunks

    return pl.pallas_call(
        functools.partial(_attn_kernel, tk=tk, fixed_chunks=chunks_for(ATTN_UNITS_PER_STEP),
                          online_chunks=chunks_for(ATTN_ONLINE_UNITS_PER_STEP)),
        grid=(batch, N_KV_HEADS, per_seq),
        in_specs=[
            pl.BlockSpec((qw, tqb), lambda b, g, i: (g, b * per_seq + i)),
            pl.BlockSpec((seq, KV_LANES), lambda b, g, i: (b, g)),
            pl.BlockSpec((VT_ROWS, seq), lambda b, g, i: (g, b)),
            pl.BlockSpec((SUBLANES, seq), lambda b, g, i: (0, b)),
        ],
        out_specs=pl.BlockSpec((tqb, qw), lambda b, g, i: (b * per_seq + i, g)),
        out_shape=jax.ShapeDtypeStruct((m, D_MODEL), BF16),
        scratch_shapes=[pltpu.VMEM((n_streams, 1, tq), F32),
                        pltpu.VMEM((n_streams, VT_ROWS, tq), F32),
                        pltpu.VMEM((ATTN_AHEAD, tk, tq), F32),
                        pltpu.VMEM((ATTN_AHEAD, tk, tq), BF16),
                        pltpu.VMEM((n_streams + 1, SUBLANES, tq), F32)],
        compiler_params=_params(3),
        name="attn",
    )(qt, k, vt, kn)


def _pool_kernel(xp_ref, x_ref, xn_ref, g_ref, win_ref, wgrp_ref, wout_ref, scale_ref,
                 o_ref, xe_ref, h_ref, z_ref, *, seq):
    tm = x_ref.shape[0]
    s0 = (pl.program_id(0) % (seq // tm)) * tm
    xe_ref[0:HALO, :] = xp_ref[...]
    xe_ref[HALO:HALO + tm, :] = x_ref[...]
    xe_ref[HALO + tm:, :] = xn_ref[...]
    h_ref[...] = _rmsnorm(xe_ref[...], g_ref[...]).astype(BF16)
    row = lax.broadcasted_iota(jnp.int32, (tm + 2 * HALO, 1), 0)
    inside = jnp.logical_and(jnp.logical_or(row >= HALO, s0 > 0),
                             jnp.logical_or(row < HALO + tm, s0 + tm < seq))
    t = s0 + lax.broadcasted_iota(jnp.int32, (tm, 1), 0)

    def in_proj(g):
        cols = slice(g * POOL_GROUP_DIM, (g + 1) * POOL_GROUP_DIM)
        return jnp.dot(h_ref[...], win_ref[:, cols], preferred_element_type=F32)

    rows = tm + 2 * HALO
    u_next = in_proj(0)
    for g, w in enumerate(POOL_WINDOWS):
        cols = slice(g * POOL_GROUP_DIM, (g + 1) * POOL_GROUP_DIM)
        ue = jnp.where(inside, u_next, 0.0)
        if g + 1 < len(POOL_WINDOWS):
            u_next = in_proj(g + 1)
        fwd, n = ue, 1
        while 2 * n < w:
            fwd = fwd + pltpu.roll(fwd, rows - n, 0)
            n *= 2
        win = fwd + pltpu.roll(fwd, w // 2, 0)
        cnt = (jnp.minimum(t + w // 2, seq) - jnp.maximum(t - w // 2, 0)).astype(F32)
        d = win[HALO:HALO + tm, :] / cnt - ue[HALO:HALO + tm, :]
        z = jnp.dot(d.astype(BF16), wgrp_ref[g], preferred_element_type=F32)
        z_ref[:, cols] = z.astype(BF16)
    y = jnp.dot(z_ref[...], wout_ref[...], preferred_element_type=F32)
    o_ref[...] = x_ref[...] + y * scale_ref[...]


def _pool(x, gains, layer, w_in, w_grp, w_out, scale, j, seq):
    m = x.shape[0]
    tm = _tile(seq, 512)
    hb = tm // HALO
    last = m // HALO - 1
    return pl.pallas_call(
        functools.partial(_pool_kernel, seq=seq),
        grid=(m // tm,),
        in_specs=[
            pl.BlockSpec((HALO, D_MODEL), lambda i: (jnp.maximum(i * hb - 1, 0), 0)),
            pl.BlockSpec((tm, D_MODEL), lambda i: (i, 0)),
            pl.BlockSpec((HALO, D_MODEL), lambda i: (jnp.minimum((i + 1) * hb, last), 0)),
            _const_spec((None, None, 1, D_MODEL), (layer, 1, 0, 0)),
            _const_spec((None, D_MODEL, D_MODEL), (j, 0, 0)),
            _const_spec((None, len(POOL_WINDOWS), POOL_GROUP_DIM, POOL_GROUP_DIM), (j, 0, 0, 0)),
            _const_spec((None, D_MODEL, D_MODEL), (j, 0, 0)),
            _const_spec((None, 1, D_MODEL), (j, 0, 0)),
        ],
        out_specs=pl.BlockSpec((tm, D_MODEL), lambda i: (i, 0)),
        out_shape=jax.ShapeDtypeStruct((m, D_MODEL), F32),
        scratch_shapes=[pltpu.VMEM((tm + 2 * HALO, D_MODEL), F32),
                        pltpu.VMEM((tm + 2 * HALO, D_MODEL), BF16),
                        pltpu.VMEM((tm, D_MODEL), BF16)],
        compiler_params=_params(1),
        name="pool",
    )(x, x, x, gains, w_in, w_grp, w_out, scale)


def _rope_tables(seq):
    rows = seq // GRID_W
    row = jnp.repeat(jnp.arange(rows, dtype=F32), GRID_W)
    col = jnp.tile(jnp.arange(GRID_W, dtype=F32), rows)
    n_freq = HEAD_DIM // 4
    freqs = ROPE_THETA ** (-jnp.arange(n_freq, dtype=F32) / n_freq)
    ang = jnp.concatenate([row[:, None] * freqs, col[:, None] * freqs], axis=-1)
    dim = jnp.arange(LANES)
    pair = (dim % HEAD_DIM) // 2
    cos = jnp.cos(ang)[:, pair].T
    sin = (jnp.sin(ang)[:, pair] * jnp.where(dim % 2 == 0, -1.0, 1.0).astype(F32)).T
    return cos, sin


def _head_gain_tile(gain):
    g = jnp.tile(gain, (1, LANES // HEAD_DIM))
    return jnp.broadcast_to(g[:, :, None], (gain.shape[0], LANES, LANES))


def _run_trunk(x, p):
    batch, seq, _ = x.shape
    x = x.reshape(batch * seq, D_MODEL)
    cos, sin = _rope_tables(seq)
    for i in range(DEPTH):
        x = _ffn(x, p["norm"], p["wg"], p["wu"], p["wd"], i, 0, 0)
        j = i // 2
        if i % 2 == 0:
            qt, k, vt, kn = _qkv(x, p["norm"], i, p["wqkv"], p["qg"], p["kg"], cos, sin, j, seq)
            mixer = (_attention(qt, k, vt, kn, batch, seq), p["wo"], j)
        else:
            x = _pool(x, p["norm"], i, p["pin"], p["pgrp"], p["pout"], p["pscale"], j, seq)
            mixer = None
        x = _ffn(x, p["norm"], p["wg"], p["wu"], p["wd"], i, 1, 2, mixer)
    return x.reshape(batch, seq, D_MODEL)


def _prepare(norm_gains, ffn_w_gate, ffn_w_up, ffn_w_down, attn_w_qkv, attn_q_gain, attn_k_gain,
             attn_w_o, pool_w_in, pool_w_group, pool_w_out, pool_scale):
    return {
        "norm": norm_gains.reshape(DEPTH, 3, 1, D_MODEL),
        "wg": ffn_w_gate.astype(BF16),
        "wu": ffn_w_up.astype(BF16),
        "wd": ffn_w_down.astype(BF16),
        "wqkv": attn_w_qkv.astype(BF16),
        "qg": _head_gain_tile(attn_q_gain * Q_SCALE),
        "kg": _head_gain_tile(attn_k_gain),
        "wo": attn_w_o.astype(BF16),
        "pin": pool_w_in.astype(BF16),
        "pgrp": pool_w_group.astype(BF16),
        "pout": pool_w_out.astype(BF16),
        "pscale": pool_scale.reshape(-1, 1, D_MODEL),
    }


def kernel(x_prompt, x_sample, norm_gains, ffn_w_gate, ffn_w_up, ffn_w_down, attn_w_qkv, attn_q_gain, attn_k_gain, attn_w_o, pool_w_in, pool_w_group, pool_w_out, pool_scale):
    p = _prepare(norm_gains, ffn_w_gate, ffn_w_up, ffn_w_down, attn_w_qkv, attn_q_gain, attn_k_gain,
                 attn_w_o, pool_w_in, pool_w_group, pool_w_out, pool_scale)
    return (_run_trunk(x_prompt, p), _run_trunk(x_sample, p))
```
